```python
import jax
import jax.numpy as jnp
from jax import lax
import numpy as np

D_MODEL = 1024
BATCH = 8
SEQ = 2048
DEPTH = 1

CTX_LEN = 256
GRID_W = 64
CHUNK = 64
EPS = 1e-6

RET_HEADS = 4
RET_DK = 256
RET_DV = 512
RET_QK = RET_HEADS * RET_DK
RET_V = RET_HEADS * RET_DV
ROPE_THETA = 10000.0
ROPE_FREQS = RET_DK // 4

SSD_INNER = 2 * D_MODEL
SSD_HEADDIM = 64
SSD_HEADS = SSD_INNER // SSD_HEADDIM
SSD_GROUPS = 8
SSD_HPG = SSD_HEADS // SSD_GROUPS
SSD_STATE = 128
SSD_BC = SSD_GROUPS * SSD_STATE
SSD_CONV_W = 5
SSD_CONV_DIM = SSD_INNER + 2 * SSD_BC
SSD_NORM_GROUP = SSD_INNER // SSD_GROUPS

N_BRANCH = 2
IN_SIZES = (RET_QK, RET_QK, RET_V, RET_V, SSD_INNER, SSD_CONV_DIM, 2 * SSD_HEADS, N_BRANCH * D_MODEL)
IN_OFFSETS = tuple(int(o) for o in np.cumsum(IN_SIZES)[:-1])
IN_DIM = int(sum(IN_SIZES))

kernel_name = 'hybrid_retention_ssd_dit_layer'


def rmsnorm(x, w):
    xf = x.astype(jnp.float32)
    return xf * lax.rsqrt(jnp.mean(xf * xf, axis=-1, keepdims=True) + EPS) * w.astype(jnp.float32)


def grid_rope_tables(n_tokens):
    rows = n_tokens // GRID_W
    row = jnp.repeat(jnp.arange(rows, dtype=jnp.float32), GRID_W)
    col = jnp.tile(jnp.arange(GRID_W, dtype=jnp.float32), rows)
    inv_freq = ROPE_THETA ** (-jnp.arange(ROPE_FREQS, dtype=jnp.float32) / ROPE_FREQS)
    ang = jnp.stack([row[:, None] * inv_freq, col[:, None] * inv_freq], axis=1)
    return jnp.cos(ang), jnp.sin(ang)


def apply_grid_rope(x, cos, sin):
    b, L, h, d = x.shape
    xb = x.reshape(b, L, h, 2, 2, ROPE_FREQS)
    x1, x2 = xb[..., 0, :], xb[..., 1, :]
    cs, sn = cos[None, :, None], sin[None, :, None]
    return jnp.stack([x1 * cs - x2 * sn, x1 * sn + x2 * cs], axis=-2).reshape(b, L, h, d)


def dwconv_centred(x, w, bias):
    ch = x.shape[-1]
    y = lax.conv_general_dilated(x, w[:, None, :].astype(x.dtype), (1,),
                                 [(SSD_CONV_W // 2, SSD_CONV_W // 2)],
                                 dimension_numbers=('NWC', 'WIO', 'NWC'), feature_group_count=ch)
    return y + bias


def chunk_scan(q, k, v, log_a, s0, include_diag):
    f32 = jnp.float32
    b, L, g, n = q.shape
    r, p = v.shape[-2:]
    nc = L // CHUNK
    qc = q.astype(f32).reshape(b, nc, CHUNK, g, n)
    kc = k.astype(f32).reshape(b, nc, CHUNK, g, n)
    vc = v.astype(f32).reshape(b, nc, CHUNK, g, r, p)
    acum = jnp.cumsum(log_a.astype(f32).reshape(b, nc, CHUNK, g, r), axis=2)
    idx = jnp.arange(CHUNK)
    mask = (idx[:, None] >= idx[None, :]) if include_diag else (idx[:, None] > idx[None, :])
    seg = acum[:, :, :, None] - acum[:, :, None, :]
    decay = jnp.exp(jnp.where(mask[:, :, None, None], seg, -jnp.inf))
    scores = jnp.einsum('bcign,bcjgn->bcijg', qc, kc)
    y_intra = jnp.einsum('bcijgr,bcjgrp->bcigrp', scores[..., None] * decay, vc)

    def step(s, inp):
        q_ch, k_ch, v_ch, a_ch = inp
        y_inter = jnp.einsum('bign,bgrpn->bigrp', q_ch, s) * jnp.exp(a_ch)[..., None]
        to_end = jnp.exp(a_ch[:, -1:] - a_ch)
        s_new = (jnp.exp(a_ch[:, -1])[..., None, None] * s
                 + jnp.einsum('bjgn,bjgrp->bgrpn', k_ch, to_end[..., None] * v_ch))
        return s_new, y_inter

    chunk_major = lambda t: jnp.moveaxis(t, 1, 0)
    s_final, y_inter = lax.scan(step, s0, (chunk_major(qc), chunk_major(kc), chunk_major(vc), chunk_major(acum)))
    y = y_intra + jnp.moveaxis(y_inter, 0, 1)
    return y.reshape(b, L, g, r, p), s_final


def bidir_scan(q_c, k_c, v_c, a_c, q_l, k_l, v_l, a_l):
    b, _, g, n = q_c.shape
    r, p = v_c[0].shape[-2:]
    s0 = jnp.zeros((b, g, r, p, n), jnp.float32)
    flip = lambda t: jnp.flip(t, axis=1)
    yc_f, sc_f = chunk_scan(q_c, k_c, v_c[0], a_c[0], s0, True)
    yl_f, _ = chunk_scan(q_l, k_l, v_l[0], a_l[0], sc_f, True)
    yc_b, sc_b = chunk_scan(flip(q_c), flip(k_c), flip(v_c[1]), flip(a_c[1]), s0, False)
    yl_b, _ = chunk_scan(flip(q_l), flip(k_l), flip(v_l[1]), flip(a_l[1]), sc_b, False)
    return yc_f + flip(yc_b), yl_f + flip(yl_b)


def retention_inputs(q, k, v, log_gamma, rope):
    b, L, _ = q.shape
    q = q.reshape(b, L, RET_HEADS, RET_DK) * (RET_DK ** -0.5)
    k = k.reshape(b, L, RET_HEADS, RET_DK)
    if rope is not None:
        q = apply_grid_rope(q, rope[0], rope[1])
        k = apply_grid_rope(k, rope[0], rope[1])
    v = v.reshape(b, L, RET_HEADS, 1, RET_DV)
    a = tuple(jnp.broadcast_to(log_gamma[d][None, None, :, None], (b, L, RET_HEADS, 1)) for d in range(2))
    return q, k, (v, v), a


def ssd_inputs(xbc, dt_raw, p):
    f32 = jnp.float32
    b, L, _ = xbc.shape
    xbc = jax.nn.silu(dwconv_centred(xbc, p['ssd_conv_w'], p['ssd_conv_b']))
    xs, bm, cm = jnp.split(xbc, [SSD_INNER, SSD_INNER + SSD_BC], axis=-1)
    xs = xs.reshape(b, L, SSD_GROUPS, SSD_HPG, SSD_HEADDIM)
    bm = bm.reshape(b, L, SSD_GROUPS, SSD_STATE)
    cm = cm.reshape(b, L, SSD_GROUPS, SSD_STATE)
    dt = jax.nn.softplus(dt_raw.astype(f32).reshape(b, L, 2, SSD_HEADS) + p['ssd_dt_bias'].astype(f32))
    A = -jnp.exp(p['ssd_a_log'].astype(f32))
    a = tuple((dt[:, :, d] * A[d]).reshape(b, L, SSD_GROUPS, SSD_HPG) for d in range(2))
    v = tuple(xs * dt[:, :, d].reshape(b, L, SSD_GROUPS, SSD_HPG)[..., None] for d in range(2))
    return cm, bm, v, a, xs


def merge_branches(y_ret, g_ret, y_ssd, x_ssd, z, gates, p):
    b, L = g_ret.shape[:2]
    yr = y_ret.reshape(b, L, RET_HEADS, RET_DV)
    mu = jnp.mean(yr, axis=-1, keepdims=True)
    var = jnp.mean(jnp.square(yr - mu), axis=-1, keepdims=True)
    yr = ((yr - mu) * lax.rsqrt(var + EPS)).reshape(b, L, RET_V) * p['ret_gn_w']
    o_ret = (yr * jax.nn.silu(g_ret)) @ p['w_ret_o']
    ys = y_ssd + p['ssd_D'].reshape(SSD_GROUPS, SSD_HPG)[:, :, None] * x_ssd
    ys = (ys.reshape(b, L, SSD_INNER) * jax.nn.silu(z)).reshape(b, L, SSD_GROUPS, SSD_NORM_GROUP)
    ys = (ys * lax.rsqrt(jnp.mean(ys * ys, axis=-1, keepdims=True) + EPS)).reshape(b, L, SSD_INNER)
    o_ssd = (ys * p['ssd_norm_w']) @ p['w_ssd_o']
    g_r, g_s = jnp.split(gates, 2, axis=-1)
    return (jax.nn.sigmoid(g_r) * o_ret + jax.nn.sigmoid(g_s) * o_ssd) @ p['w_out']


def hybrid_layer(x, xc, mod_l, mod_c, p, rope, update_ctx):
    shift_l, scale_l, gate_l = jnp.split(mod_l[:, None, :], 3, axis=-1)
    shift_c, scale_c, gate_c = jnp.split(mod_c[None, None, :], 3, axis=-1)
    h_l = rmsnorm(x, p['norm_pre']) * (1.0 + scale_l) + shift_l
    h_c = rmsnorm(xc, p['norm_pre']) * (1.0 + scale_c) + shift_c
    q_l, k_l, v_l, g_l, z_l, xbc_l, dt_l, gt_l = jnp.split(h_l @ p['w_in'], IN_OFFSETS, axis=-1)
    q_c, k_c, v_c, g_c, z_c, xbc_c, dt_c, gt_c = jnp.split(h_c @ p['w_in'], IN_OFFSETS, axis=-1)

    log_gamma = jax.nn.log_sigmoid(p['ret_decay'].astype(jnp.float32))
    rq_c, rk_c, rv_c, ra_c = retention_inputs(q_c, k_c, v_c, log_gamma, None)
    rq_l, rk_l, rv_l, ra_l = retention_inputs(q_l, k_l, v_l, log_gamma, rope)
    yr_c, yr_l = bidir_scan(rq_c, rk_c, rv_c, ra_c, rq_l, rk_l, rv_l, ra_l)

    sc_c, sb_c, sv_c, sa_c, sx_c = ssd_inputs(xbc_c, dt_c, p)
    sc_l, sb_l, sv_l, sa_l, sx_l = ssd_inputs(xbc_l, dt_l, p)
    ys_c, ys_l = bidir_scan(sc_c, sb_c, sv_c, sa_c, sc_l, sb_l, sv_l, sa_l)

    out_l = merge_branches(yr_l, g_l, ys_l, sx_l, z_l, gt_l, p)
    x_new = (x + gate_l * rmsnorm(out_l, p['norm_post'])).astype(x.dtype)
    if update_ctx:
        out_c = merge_branches(yr_c, g_c, ys_c, sx_c, z_c, gt_c, p)
        xc = (xc + gate_c * rmsnorm(out_c, p['norm_post'])).astype(xc.dtype)
    return x_new, xc


def setup_inputs(seed: int = 0) -> dict:
    key = jax.random.key(seed)
    ks = jax.random.split(key, 20)
    nrm = lambda k, shape, scale: jax.random.normal(k, shape, jnp.float32) * scale
    x = nrm(ks[0], (BATCH, SEQ, D_MODEL), 1.0)
    c = nrm(ks[1], (BATCH, D_MODEL), 1.0)
    ctx = nrm(ks[2], (BATCH, CTX_LEN, D_MODEL), 1.0)
    c_ctx = nrm(ks[3], (D_MODEL,), 1.0)
    w_mod = nrm(ks[4], (DEPTH, D_MODEL, 3 * D_MODEL), 0.5 * D_MODEL ** -0.5)
    b_mod = nrm(ks[5], (DEPTH, 3 * D_MODEL), 0.02)
    norm_pre_w = 1.0 + nrm(ks[6], (DEPTH, D_MODEL), 0.02)
    norm_post_w = 1.0 + nrm(ks[7], (DEPTH, D_MODEL), 0.02)
    w_in = nrm(ks[8], (DEPTH, D_MODEL, IN_DIM), D_MODEL ** -0.5)
    gamma0 = 1.0 - 2.0 ** (-5.0 - np.arange(RET_HEADS))
    ret_decay = (jnp.asarray(np.log(gamma0 / (1.0 - gamma0)), jnp.float32)[None, None, :]
                 + nrm(ks[9], (DEPTH, 2, RET_HEADS), 0.1))
    ret_gn_w = 1.0 + nrm(ks[10], (DEPTH, RET_V), 0.02)
    ssd_conv_w = nrm(ks[11], (DEPTH, SSD_CONV_W, SSD_CONV_DIM), SSD_CONV_W ** -0.5)
    ssd_conv_b = nrm(ks[12], (DEPTH, SSD_CONV_DIM), 0.02)
    dt0 = jnp.exp(jax.random.uniform(ks[13], (DEPTH, 2, SSD_HEADS), jnp.float32,
                                     float(np.log(1e-3)), float(np.log(1e-1))))
    ssd_dt_bias = dt0 + jnp.log(-jnp.expm1(-dt0))
    ssd_a_log = jnp.log(jax.random.uniform(ks[14], (DEPTH, 2, SSD_HEADS), jnp.float32, 1.0, 16.0))
    ssd_D = 1.0 + nrm(ks[15], (DEPTH, SSD_HEADS), 0.1)
    ssd_norm_w = 1.0 + nrm(ks[16], (DEPTH, SSD_INNER), 0.02)
    w_ret_o = nrm(ks[17], (DEPTH, RET_V, D_MODEL), RET_V ** -0.5)
    w_ssd_o = nrm(ks[18], (DEPTH, SSD_INNER, D_MODEL), SSD_INNER ** -0.5)
    w_out = nrm(ks[19], (DEPTH, D_MODEL, D_MODEL), D_MODEL ** -0.5)
    return {'x': x, 'c': c, 'ctx': ctx, 'c_ctx': c_ctx, 'w_mod': w_mod, 'b_mod': b_mod,
            'norm_pre_w': norm_pre_w, 'norm_post_w': norm_post_w, 'w_in': w_in,
            'ret_decay': ret_decay, 'ret_gn_w': ret_gn_w, 'ssd_conv_w': ssd_conv_w,
            'ssd_conv_b': ssd_conv_b, 'ssd_dt_bias': ssd_dt_bias, 'ssd_a_log': ssd_a_log,
            'ssd_D': ssd_D, 'ssd_norm_w': ssd_norm_w, 'w_ret_o': w_ret_o, 'w_ssd_o': w_ssd_o,
            'w_out': w_out}


def reference(x, c, ctx, c_ctx, w_mod, b_mod, norm_pre_w, norm_post_w, w_in, ret_decay, ret_gn_w,
              ssd_conv_w, ssd_conv_b, ssd_dt_bias, ssd_a_log, ssd_D, ssd_norm_w, w_ret_o, w_ssd_o, w_out):
    rope = grid_rope_tables(x.shape[1])
    silu_c = jax.nn.silu(c.astype(jnp.float32))
    silu_cc = jax.nn.silu(c_ctx.astype(jnp.float32))
    xl, xc = x, ctx
    for layer in range(DEPTH):
        p = {'norm_pre': norm_pre_w[layer], 'norm_post': norm_post_w[layer], 'w_in': w_in[layer],
             'ret_decay': ret_decay[layer], 'ret_gn_w': ret_gn_w[layer],
             'ssd_conv_w': ssd_conv_w[layer], 'ssd_conv_b': ssd_conv_b[layer],
             'ssd_dt_bias': ssd_dt_bias[layer], 'ssd_a_log': ssd_a_log[layer], 'ssd_D': ssd_D[layer],
             'ssd_norm_w': ssd_norm_w[layer], 'w_ret_o': w_ret_o[layer], 'w_ssd_o': w_ssd_o[layer],
             'w_out': w_out[layer]}
        mod_l = silu_c @ w_mod[layer] + b_mod[layer]
        mod_c = silu_cc @ w_mod[layer] + b_mod[layer]
        xl, xc = hybrid_layer(xl, xc, mod_l, mod_c, p, rope, layer < DEPTH - 1)
    return xl
```

```python
import functools

import jax
import jax.numpy as jnp
from jax import lax
from jax.experimental import pallas as pl
from jax.experimental.pallas import tpu as pltpu

F32 = jnp.float32
BF16 = jnp.bfloat16
HIGHEST = lax.Precision.HIGHEST

D_MODEL = 1024
EPS = 1e-6
GRID_W = 64
ROPE_THETA = 10000.0
ROPE_FREQS = 64

RET_HEADS = 4
RET_DK = 256
RET_DV = 512
RET_QK = RET_HEADS * RET_DK
RET_V = RET_HEADS * RET_DV
RET_CHUNK = 256

SSD_INNER = 2048
SSD_HEADDIM = 64
SSD_HEADS = 32
SSD_GROUPS = 8
SSD_HPG = 4
SSD_STATE = 128
SSD_BC = SSD_GROUPS * SSD_STATE
SSD_CONV_W = 5
SSD_GW = SSD_HPG * SSD_HEADDIM
SSD_CHUNK = 128
CONV_ROWS = 256
DT_LANES = 128

IN_SIZES = (RET_QK, RET_QK, RET_V, RET_V, SSD_INNER, SSD_INNER + 2 * SSD_BC, 2 * SSD_HEADS, 2 * D_MODEL)

COL_Q = 0
COL_K = COL_Q + RET_QK
COL_V = COL_K + RET_QK
COL_XS = COL_V + RET_V
COL_B = COL_XS + SSD_INNER
COL_C = COL_B + SSD_BC
N_CTX = COL_C + SSD_BC
COL_G = N_CTX
COL_Z = COL_G + RET_V
COL_GATES = COL_Z + SSD_INNER
N_MAIN = COL_GATES + 2 * D_MODEL

VMEM_LIMIT = 48 * 1024 * 1024

NT_DIMS = (((1,), (1,)), ((), ()))
TN_DIMS = (((0,), (0,)), ((), ()))


def _silu(x):
    return x * jax.nn.sigmoid(x)


def _softplus(x):
    return jnp.maximum(x, 0.0) + jnp.log1p(jnp.exp(-jnp.abs(x)))


def _log_sigmoid(x):
    return jnp.minimum(x, 0.0) - jnp.log1p(jnp.exp(-jnp.abs(x)))


def _mod_kernel(c_ref, w_ref, b_ref, o_ref):
    o_ref[...] = jnp.dot(_silu(c_ref[...]), w_ref[...], preferred_element_type=F32, precision=HIGHEST) + b_ref[...]


def _modulation(c_all, w_mod, b_mod):
    rows = c_all.shape[0]
    n = w_mod.shape[1]
    tn = D_MODEL
    return pl.pallas_call(
        _mod_kernel,
        out_shape=jax.ShapeDtypeStruct((rows, n), F32),
        grid=(n // tn,),
        in_specs=[pl.BlockSpec((rows, D_MODEL), lambda j: (0, 0)),
                  pl.BlockSpec((D_MODEL, tn), lambda j: (0, j)),
                  pl.BlockSpec((1, tn), lambda j: (0, j))],
        out_specs=pl.BlockSpec((rows, tn), lambda j: (0, j)),
        compiler_params=pltpu.CompilerParams(dimension_semantics=("arbitrary",), vmem_limit_bytes=VMEM_LIMIT),
        name="mod",
    )(c_all, w_mod, b_mod)


def _inproj_kernel(x_ref, shift_ref, scale_ref, nw_ref, w_ref, wdt_ref, o_ref, dt_ref, h_ref):
    @pl.when(pl.program_id(1) == 0)
    def _():
        x = x_ref[...]
        r = lax.rsqrt(jnp.mean(x * x, axis=-1, keepdims=True) + EPS)
        h = (x * r * nw_ref[...] * (1.0 + scale_ref[...]) + shift_ref[...]).astype(BF16)
        h_ref[...] = h
        dt_ref[...] = jnp.dot(h, wdt_ref[...], preferred_element_type=F32)

    o_ref[...] = jnp.dot(h_ref[...], w_ref[...], preferred_element_type=F32).astype(o_ref.dtype)


def _inproj(x2, shift, scale, nw, w, wdt, seq_len, n_cols, name):
    m = x2.shape[0]
    tm = min(1024, seq_len)
    tn = 1024
    per = seq_len // tm
    nb = shift.shape[0]
    mod_idx = (lambda i, j: (i // per, 0, 0)) if nb > 1 else (lambda i, j: (0, 0, 0))
    return pl.pallas_call(
        _inproj_kernel,
        out_shape=(jax.ShapeDtypeStruct((m, n_cols), BF16), jax.ShapeDtypeStruct((m, DT_LANES), F32)),
        grid=(m // tm, n_cols // tn),
        in_specs=[pl.BlockSpec((tm, D_MODEL), lambda i, j: (i, 0)),
                  pl.BlockSpec((None, 1, D_MODEL), mod_idx),
                  pl.BlockSpec((None, 1, D_MODEL), mod_idx),
                  pl.BlockSpec((1, D_MODEL), lambda i, j: (0, 0)),
                  pl.BlockSpec((D_MODEL, tn), lambda i, j: (0, j)),
                  pl.BlockSpec((D_MODEL, DT_LANES), lambda i, j: (0, 0))],
        out_specs=(pl.BlockSpec((tm, tn), lambda i, j: (i, j)),
                   pl.BlockSpec((tm, DT_LANES), lambda i, j: (i, 0))),
        scratch_shapes=[pltpu.VMEM((tm, D_MODEL), BF16)],
        compiler_params=pltpu.CompilerParams(dimension_semantics=("parallel", "arbitrary"), vmem_limit_bytes=VMEM_LIMIT),
        name=name,
    )(x2, shift, scale, nw, w, wdt)


def _ret_kernel(q_ref, k_ref, v_ref, g_ref, kc_ref, vc_ref, cos_ref, sin_ref, dec_ref, gnw_ref, o_ref,
                qs_ref, ks_ref, sb_ref, st_ref, dm_ref):
    seq = q_ref.shape[0]
    ch = RET_CHUNK
    n_ch = seq // ch

    lg = _log_sigmoid(dec_ref[...])
    lgf = lg[0:1, 0:1]
    lgb = lg[1:2, 0:1]

    ii = lax.broadcasted_iota(jnp.int32, (ch, ch), 0)
    jj = lax.broadcasted_iota(jnp.int32, (ch, ch), 1)
    dm_ref[...] = jnp.exp(jnp.abs(ii - jj).astype(F32) * jnp.where(ii >= jj, lgf, lgb))

    pos = lax.broadcasted_iota(jnp.int32, (ch, 1), 0).astype(F32)
    in_f = jnp.exp((pos + 1.0) * lgf)
    in_b = jnp.exp((ch - pos) * lgb)
    end_f = jnp.exp((ch - 1.0 - pos) * lgf)
    end_b = jnp.exp(pos * lgb)
    all_f = jnp.exp(ch * lgf)
    all_b = jnp.exp(ch * lgb)

    def rope_rows(c, _):
        r0 = pl.multiple_of(c * ch, ch)
        for src, dst, scale in ((q_ref, qs_ref, RET_DK ** -0.5), (k_ref, ks_ref, 1.0)):
            x = src[pl.ds(r0, ch), :].astype(F32)
            parts = []
            for s in (0, 128):
                xh = x[:, s:s + 128]
                parts.append(xh * cos_ref[pl.ds(r0, ch), s:s + 128]
                             + pltpu.roll(xh, 64, 1) * sin_ref[pl.ds(r0, ch), s:s + 128])
            dst[pl.ds(r0, ch), :] = (jnp.concatenate(parts, axis=1) * scale).astype(BF16)
        return 0

    lax.fori_loop(0, n_ch, rope_rows, 0)

    def kv_outer(kb, vb, w):
        vw = (vb.astype(F32) * w).astype(BF16)
        return lax.dot_general(kb, vw, TN_DIMS, preferred_element_type=F32)

    st_ref[...] = kv_outer(kc_ref[...], vc_ref[...], end_b)

    def bwd(t, _):
        c = n_ch - 1 - t
        r0 = pl.multiple_of(c * ch, ch)
        sb_ref[c] = st_ref[...].astype(BF16)
        st_ref[...] = st_ref[...] * all_b + kv_outer(ks_ref[pl.ds(r0, ch), :], v_ref[pl.ds(r0, ch), :], end_b)
        return 0

    lax.fori_loop(0, n_ch, bwd, 0)

    st_ref[...] = kv_outer(kc_ref[...], vc_ref[...], end_f)

    def fwd(c, _):
        r0 = pl.multiple_of(c * ch, ch)
        qb = qs_ref[pl.ds(r0, ch), :]
        kb = ks_ref[pl.ds(r0, ch), :]
        vb = v_ref[pl.ds(r0, ch), :]
        sc = lax.dot_general(qb, kb, NT_DIMS, preferred_element_type=F32)
        y = jnp.dot((sc * dm_ref[...]).astype(BF16), vb, preferred_element_type=F32)
        y = y + in_f * jnp.dot(qb, st_ref[...].astype(BF16), preferred_element_type=F32)
        y = y + in_b * jnp.dot(qb, sb_ref[c], preferred_element_type=F32)
        mu = jnp.mean(y, axis=-1, keepdims=True)
        yc = y - mu
        var = jnp.mean(yc * yc, axis=-1, keepdims=True)
        g = g_ref[pl.ds(r0, ch), :].astype(F32)
        o_ref[pl.ds(r0, ch), :] = (yc * lax.rsqrt(var + EPS) * gnw_ref[...] * _silu(g)).astype(o_ref.dtype)
        st_ref[...] = st_ref[...] * all_f + kv_outer(kb, vb, end_f)
        return 0

    lax.fori_loop(0, n_ch, fwd, 0)


def _retention(proj_l, proj_c, cos_t, sin_t, dec, gnw):
    b, seq, _ = proj_l.shape
    ctx_len = proj_c.shape[1]
    assert ctx_len == RET_CHUNK and seq % RET_CHUNK == 0
    n_ch = seq // RET_CHUNK
    kq, kv = RET_DK, RET_DV
    return pl.pallas_call(
        _ret_kernel,
        out_shape=jax.ShapeDtypeStruct((b, seq, RET_V), BF16),
        grid=(b, RET_HEADS),
        in_specs=[pl.BlockSpec((None, seq, kq), lambda i, h: (i, 0, COL_Q // kq + h)),
                  pl.BlockSpec((None, seq, kq), lambda i, h: (i, 0, COL_K // kq + h)),
                  pl.BlockSpec((None, seq, kv), lambda i, h: (i, 0, COL_V // kv + h)),
                  pl.BlockSpec((None, seq, kv), lambda i, h: (i, 0, COL_G // kv + h)),
                  pl.BlockSpec((None, ctx_len, kq), lambda i, h: (i, 0, COL_K // kq + h)),
                  pl.BlockSpec((None, ctx_len, kv), lambda i, h: (i, 0, COL_V // kv + h)),
                  pl.BlockSpec((seq, kq), lambda i, h: (0, 0)),
                  pl.BlockSpec((seq, kq), lambda i, h: (0, 0)),
                  pl.BlockSpec((None, 2, 128), lambda i, h: (h, 0, 0)),
                  pl.BlockSpec((1, kv), lambda i, h: (0, h))],
        out_specs=pl.BlockSpec((None, seq, kv), lambda i, h: (i, 0, h)),
        scratch_shapes=[pltpu.VMEM((seq, kq), BF16), pltpu.VMEM((seq, kq), BF16),
                        pltpu.VMEM((n_ch, kq, kv), BF16), pltpu.VMEM((kq, kv), F32),
                        pltpu.VMEM((RET_CHUNK, RET_CHUNK), F32)],
        compiler_params=pltpu.CompilerParams(dimension_semantics=("parallel", "arbitrary"), vmem_limit_bytes=VMEM_LIMIT),
        name="ret",
    )(proj_l, proj_l, proj_l, proj_l, proj_c, proj_c, cos_t, sin_t, dec, gnw)


def _split_bf16(a, parts):
    out = []
    for _ in range(parts):
        p = a.astype(BF16)
        out.append(p)
        a = a - p.astype(F32)
    return out


def _expand_heads(t, d, lane):
    c = [t[:, 4 * d + r:4 * d + r + 1] for r in range(SSD_HPG)]
    return jnp.where(lane < 64, c[0], jnp.where(lane < 128, c[1], jnp.where(lane < 192, c[2], c[3])))


def _ssd_kernel(xs_ref, b_ref, c_ref, z_ref, dt_ref, xsc_ref, bc_ref, dtc_ref,
                cwx_ref, cwb_ref, cwc_ref, cbx_ref, cbb_ref, cbc_ref, bias_ref, alog_ref, dsk_ref, nw_ref, o_ref,
                pad_ref, xc_ref, bs_ref, cs_ref, dtv_ref, av_ref,
                padc_ref, xcc_ref, bsc_ref, dtvc_ref, avc_ref, sb_ref, sf_ref, sr_ref, tri_ref, sel_ref):
    seq = xs_ref.shape[0]
    ctx_len = xsc_ref.shape[0]
    q = SSD_CHUNK
    n_q = seq // q
    n_qc = ctx_len // q
    gw = SSD_GW
    ns = SSD_STATE
    grp = pl.program_id(1)

    ti = lax.broadcasted_iota(jnp.int32, (q, 2 * q), 0)
    tj = lax.broadcasted_iota(jnp.int32, (q, 2 * q), 1)
    tri_ref[...] = jnp.where(jnp.where(tj >= q, tj - q, tj) <= ti, 1.0, 0.0).astype(BF16)
    sr = lax.broadcasted_iota(jnp.int32, (16, 4 * 128), 0)
    scol = lax.broadcasted_iota(jnp.int32, (16, 4 * 128), 1)
    piece = scol // 128
    sl = scol - piece * 128
    want_lane = jnp.where(sr < 8, sr, sr - 8)
    cum_row = jnp.where(sr < 8, 1, 0)
    cum_piece = jnp.where(piece < 3, 1, 0)
    sel_ref[...] = jnp.where((sl == want_lane) & (cum_row == cum_piece), 1.0, 0.0).astype(BF16)

    lane128 = lax.broadcasted_iota(jnp.int32, (q, 128), 1)
    lane_q = lax.broadcasted_iota(jnp.int32, (q, gw), 1)
    lane_1 = lax.broadcasted_iota(jnp.int32, (1, gw), 1)

    shift = (128 - 8 * grp) % 128
    bias = bias_ref[...]
    neg_a = -jnp.exp(alog_ref[...])

    def prep(pad, srcs, weights, biases, dts, nrows, dsts, dtv_dst, av_dst):
        width = pad.shape[1]
        pad[0:8, :] = jnp.zeros((8, width), F32)
        pad[nrows + 8:nrows + 16, :] = jnp.zeros((8, width), F32)
        cw = jnp.concatenate([w[...] for w in weights], axis=1)
        cb = jnp.concatenate([b_[...] for b_ in biases], axis=1)
        rows = min(CONV_ROWS, nrows)

        def fill(i, _):
            r0 = pl.multiple_of(i * rows, rows)
            off = 0
            for s in srcs:
                w_ = s.shape[1]
                pad[pl.ds(r0 + 8, rows), off:off + w_] = s[pl.ds(r0, rows), :].astype(F32)
                off += w_
            d = pltpu.roll(dts[pl.ds(r0, rows), :], shift, 1)
            dv = _softplus(d + bias)
            dtv_dst[pl.ds(r0, rows), :] = dv
            av_dst[pl.ds(r0, rows), :] = dv * neg_a
            return 0

        lax.fori_loop(0, nrows // rows, fill, 0)

        def conv(i, _):
            r0 = pl.multiple_of(i * rows, rows)
            v = pad[pl.ds(r0, rows + 16), :]
            acc = cb + cw[0:1, :] * v[6:6 + rows, :]
            for k in range(1, SSD_CONV_W):
                acc = acc + cw[k:k + 1, :] * v[6 + k:6 + k + rows, :]
            y = _silu(acc)
            off = 0
            for dst in dsts:
                w_ = dst.shape[1]
                dst[pl.ds(r0, rows), :] = y[:, off:off + w_].astype(dst.dtype)
                off += w_
            return 0

        lax.fori_loop(0, nrows // rows, conv, 0)

    prep(pad_ref, (xs_ref, b_ref, c_ref), (cwx_ref, cwb_ref, cwc_ref), (cbx_ref, cbb_ref, cbc_ref),
         dt_ref, seq, (xc_ref, bs_ref, cs_ref), dtv_ref, av_ref)
    prep(padc_ref, (xsc_ref, bc_ref), (cwx_ref, cwb_ref), (cbx_ref, cbb_ref),
         dtc_ref, ctx_len, (xcc_ref, bsc_ref), dtvc_ref, avc_ref)

    def col_tables(a):
        hi, lo = _split_bf16(a, 2)
        p = jnp.dot(tri_ref[...], jnp.concatenate([hi, lo], axis=0), preferred_element_type=F32)
        tot = p[q - 1:q, :]
        return jnp.where(lane128 < 4, p, tot - p + a), tot

    def state_step(s_ref, d, xcf, bb, dtv, cum, tot):
        coef = dtv * jnp.exp(tot - cum)
        xw = (xcf * _expand_heads(coef, d, lane_q)).astype(BF16)
        upd = lax.dot_general(bb, xw, TN_DIMS, preferred_element_type=F32)
        s_ref[...] = s_ref[...] * _expand_heads(jnp.exp(tot), d, lane_1) + upd

    def ctx_step(s_ref, d, cc):
        r0 = cc * q
        cum, tot = col_tables(avc_ref[r0:r0 + q, :])
        state_step(s_ref, d, xcc_ref[r0:r0 + q, :], bsc_ref[r0:r0 + q, :], dtvc_ref[r0:r0 + q, :], cum, tot)

    sr_ref[...] = jnp.zeros((ns, gw), F32)
    for cc in reversed(range(n_qc)):
        ctx_step(sr_ref, 1, cc)

    def bwd(t, _):
        c = n_q - 1 - t
        r0 = pl.multiple_of(c * q, q)
        sb_ref[c] = sr_ref[...].astype(BF16)
        cum, tot = col_tables(av_ref[pl.ds(r0, q), :])
        state_step(sr_ref, 1, xc_ref[pl.ds(r0, q), :], bs_ref[pl.ds(r0, q), :], dtv_ref[pl.ds(r0, q), :], cum, tot)
        return 0

    lax.fori_loop(0, n_q, bwd, 0)

    sf_ref[...] = jnp.zeros((ns, gw), F32)
    for cc in range(n_qc):
        ctx_step(sf_ref, 0, cc)

    ii = lax.broadcasted_iota(jnp.int32, (q, q), 0)
    jj = lax.broadcasted_iota(jnp.int32, (q, q), 1)
    dsk = dsk_ref[...]
    nw = nw_ref[...]

    def fwd(c, _):
        r0 = pl.multiple_of(c * q, q)
        lower = ii >= jj
        dtv = dtv_ref[pl.ds(r0, q), :]
        cum, tot = col_tables(av_ref[pl.ds(r0, q), :])
        pieces = _split_bf16(cum, 3) + _split_bf16(dtv, 1)
        rows = lax.dot_general(sel_ref[...], jnp.concatenate(pieces, axis=1), NT_DIMS, preferred_element_type=F32)

        cb = cs_ref[pl.ds(r0, q), :]
        bb = bs_ref[pl.ds(r0, q), :]
        xcf = xc_ref[pl.ds(r0, q), :]
        xcb = xcf.astype(BF16)
        s = lax.dot_general(cb, bb, NT_DIMS, preferred_element_type=F32)
        ws = []
        xm = []
        for r in range(SSD_HPG):
            arg = jnp.where(lower, cum[:, r:r + 1] - rows[r:r + 1, :], cum[:, 4 + r:5 + r] - rows[4 + r:5 + r, :])
            dts = jnp.where(lower, rows[8 + r:9 + r, :], rows[12 + r:13 + r, :])
            ws.append((s * jnp.exp(arg) * dts).astype(BF16))
            xm.append(jnp.where((lane_q >= 64 * r) & (lane_q < 64 * r + 64), xcb, jnp.zeros_like(xcb)))
        y = jnp.dot(jnp.concatenate(ws, axis=1), jnp.concatenate(xm, axis=0), preferred_element_type=F32)

        s_all = jnp.concatenate([sf_ref[...].astype(BF16), sb_ref[c]], axis=1)
        yi = jnp.dot(cb, s_all, preferred_element_type=F32)
        ecum = jnp.exp(cum)
        y = y + yi[:, :gw] * _expand_heads(ecum, 0, lane_q) + yi[:, gw:] * _expand_heads(ecum, 1, lane_q)

        y = (y + dsk * xcf) * _silu(z_ref[pl.ds(r0, q), :].astype(F32))
        y = y * lax.rsqrt(jnp.mean(y * y, axis=-1, keepdims=True) + EPS) * nw
        o_ref[pl.ds(r0, q), :] = y.astype(o_ref.dtype)

        state_step(sf_ref, 0, xcf, bb, dtv, cum, tot)
        return 0

    lax.fori_loop(0, n_q, fwd, 0)


def _ssd(proj_l, proj_c, dt_l, dt_c, conv_w, conv_b, bias, alog, dsk, nw):
    b, seq, _ = proj_l.shape
    ctx_len = proj_c.shape[1]
    assert seq % CONV_ROWS == 0 and ctx_len % SSD_CHUNK == 0
    gw, ns, q = SSD_GW, SSD_STATE, SSD_CHUNK
    cx, cb_, cc_ = 0, SSD_INNER // ns, (SSD_INNER + SSD_BC) // ns
    return pl.pallas_call(
        _ssd_kernel,
        out_shape=jax.ShapeDtypeStruct((b, seq, SSD_INNER), BF16),
        grid=(b, SSD_GROUPS),
        in_specs=[pl.BlockSpec((None, seq, gw), lambda i, g: (i, 0, COL_XS // gw + g)),
                  pl.BlockSpec((None, seq, ns), lambda i, g: (i, 0, COL_B // ns + g)),
                  pl.BlockSpec((None, seq, ns), lambda i, g: (i, 0, COL_C // ns + g)),
                  pl.BlockSpec((None, seq, gw), lambda i, g: (i, 0, COL_Z // gw + g)),
                  pl.BlockSpec((None, seq, DT_LANES), lambda i, g: (i, 0, 0)),
                  pl.BlockSpec((None, ctx_len, gw), lambda i, g: (i, 0, COL_XS // gw + g)),
                  pl.BlockSpec((None, ctx_len, ns), lambda i, g: (i, 0, COL_B // ns + g)),
                  pl.BlockSpec((None, ctx_len, DT_LANES), lambda i, g: (i, 0, 0)),
                  pl.BlockSpec((SSD_CONV_W, gw), lambda i, g: (0, cx + g)),
                  pl.BlockSpec((SSD_CONV_W, ns), lambda i, g: (0, cb_ + g)),
                  pl.BlockSpec((SSD_CONV_W, ns), lambda i, g: (0, cc_ + g)),
                  pl.BlockSpec((1, gw), lambda i, g: (0, cx + g)),
                  pl.BlockSpec((1, ns), lambda i, g: (0, cb_ + g)),
                  pl.BlockSpec((1, ns), lambda i, g: (0, cc_ + g)),
                  pl.BlockSpec((None, 1, DT_LANES), lambda i, g: (g, 0, 0)),
                  pl.BlockSpec((None, 1, DT_LANES), lambda i, g: (g, 0, 0)),
                  pl.BlockSpec((1, gw), lambda i, g: (0, g)),
                  pl.BlockSpec((1, gw), lambda i, g: (0, g))],
        out_specs=pl.BlockSpec((None, seq, gw), lambda i, g: (i, 0, g)),
        scratch_shapes=[pltpu.VMEM((seq + 16, gw + 2 * ns), F32), pltpu.VMEM((seq, gw), F32),
                        pltpu.VMEM((seq, ns), BF16), pltpu.VMEM((seq, ns), BF16),
                        pltpu.VMEM((seq, DT_LANES), F32), pltpu.VMEM((seq, DT_LANES), F32),
                        pltpu.VMEM((ctx_len + 16, gw + ns), F32), pltpu.VMEM((ctx_len, gw), F32),
                        pltpu.VMEM((ctx_len, ns), BF16),
                        pltpu.VMEM((ctx_len, DT_LANES), F32), pltpu.VMEM((ctx_len, DT_LANES), F32),
                        pltpu.VMEM((seq // q, ns, gw), BF16), pltpu.VMEM((ns, gw), F32), pltpu.VMEM((ns, gw), F32),
                        pltpu.VMEM((q, 2 * q), BF16), pltpu.VMEM((16, 4 * 128), BF16)],
        compiler_params=pltpu.CompilerParams(dimension_semantics=("parallel", "arbitrary"), vmem_limit_bytes=VMEM_LIMIT),
        name="ssd",
    )(proj_l, proj_l, proj_l, proj_l, dt_l, proj_c, proj_c, dt_c,
      conv_w, conv_w, conv_w, conv_b, conv_b, conv_b, bias, alog, dsk, nw)


def _merge_kernel(yr_ref, ys_ref, gt_ref, x_ref, gate_ref, wr_ref, ws_ref, wo_ref, nw_ref, o_ref):
    o_r = jnp.dot(yr_ref[...], wr_ref[...], preferred_element_type=F32)
    o_s = jnp.dot(ys_ref[...], ws_ref[...], preferred_element_type=F32)
    gt = gt_ref[...].astype(F32)
    m = jax.nn.sigmoid(gt[:, :D_MODEL]) * o_r + jax.nn.sigmoid(gt[:, D_MODEL:]) * o_s
    out = jnp.dot(m.astype(BF16), wo_ref[...], preferred_element_type=F32)
    nrm = out * lax.rsqrt(jnp.mean(out * out, axis=-1, keepdims=True) + EPS) * nw_ref[...]
    o_ref[...] = x_ref[...] + gate_ref[...] * nrm


def _merge(yr, ys, proj_l, x2, gate, w_ret_o, w_ssd_o, w_out, nw, seq_len):
    m = x2.shape[0]
    tm = min(512, seq_len)
    per = seq_len // tm
    const = dict(pipeline_mode=pl.Buffered(1))
    return pl.pallas_call(
        _merge_kernel,
        out_shape=jax.ShapeDtypeStruct((m, D_MODEL), F32),
        grid=(m // tm,),
        in_specs=[pl.BlockSpec((tm, RET_V), lambda i: (i, 0)),
                  pl.BlockSpec((tm, SSD_INNER), lambda i: (i, 0)),
                  pl.BlockSpec((tm, 2 * D_MODEL), lambda i: (i, COL_GATES // (2 * D_MODEL))),
                  pl.BlockSpec((tm, D_MODEL), lambda i: (i, 0)),
                  pl.BlockSpec((None, 1, D_MODEL), lambda i: (i // per, 0, 0)),
                  pl.BlockSpec((RET_V, D_MODEL), lambda i: (0, 0), **const),
                  pl.BlockSpec((SSD_INNER, D_MODEL), lambda i: (0, 0), **const),
                  pl.BlockSpec((D_MODEL, D_MODEL), lambda i: (0, 0), **const),
                  pl.BlockSpec((1, D_MODEL), lambda i: (0, 0))],
        out_specs=pl.BlockSpec((tm, D_MODEL), lambda i: (i, 0)),
        compiler_params=pltpu.CompilerParams(dimension_semantics=("parallel",), vmem_limit_bytes=VMEM_LIMIT),
        name="merge",
    )(yr, ys, proj_l, x2, gate, w_ret_o, w_ssd_o, w_out, nw)


def _rope_tables(n_tokens):
    pos = jnp.arange(n_tokens)
    row = (pos // GRID_W).astype(F32)
    col = (pos % GRID_W).astype(F32)
    inv_freq = ROPE_THETA ** (-jnp.arange(ROPE_FREQS, dtype=F32) / ROPE_FREQS)
    ar = row[:, None] * inv_freq
    ac = col[:, None] * inv_freq
    cos_t = jnp.concatenate([jnp.cos(ar), jnp.cos(ar), jnp.cos(ac), jnp.cos(ac)], axis=1)
    sin_t = jnp.concatenate([-jnp.sin(ar), jnp.sin(ar), -jnp.sin(ac), jnp.sin(ac)], axis=1)
    return cos_t, sin_t


def _group_major(t):
    lead = t.shape[:-2]
    t = t.reshape(lead + (2, SSD_GROUPS, SSD_HPG))
    t = jnp.moveaxis(t, -3, -2)
    return t.reshape(lead + (SSD_GROUPS, 2 * SSD_HPG))


def kernel(x, c, ctx, c_ctx, w_mod, b_mod, norm_pre_w, norm_post_w, w_in, ret_decay, ret_gn_w, ssd_conv_w, ssd_conv_b,
           ssd_dt_bias, ssd_a_log, ssd_D, ssd_norm_w, w_ret_o, w_ssd_o, w_out):
    assert w_in.shape[0] == 1, "single-layer problem: the context stream is never updated"
    b, seq, d = x.shape
    ctx_len = ctx.shape[1]

    offs = [0]
    for s in IN_SIZES:
        offs.append(offs[-1] + s)
    wq, wk, wv, wg, wz, wxbc, wdt, wgt = [w_in[0][:, offs[i]:offs[i + 1]] for i in range(len(IN_SIZES))]
    w_main = jnp.concatenate([wq, wk, wv, wxbc, wg, wz, wgt], axis=1).astype(BF16)
    wdt = _group_major(wdt.reshape(d, 2, SSD_HEADS)).reshape(d, 2 * SSD_HEADS)
    wdt = jnp.pad(wdt, ((0, 0), (0, DT_LANES - 2 * SSD_HEADS))).astype(BF16)
    pad8 = ((0, 0), (0, DT_LANES - 2 * SSD_HPG))
    bias = jnp.pad(_group_major(ssd_dt_bias[0]), pad8)[:, None, :]
    alog = jnp.pad(_group_major(ssd_a_log[0]), pad8)[:, None, :]
    dsk = jnp.repeat(ssd_D[0], SSD_HEADDIM)[None, :]
    dec = jnp.broadcast_to(ret_decay[0].T[:, :, None], (RET_HEADS, 2, 128))

    n_rows = -(-(b + 1) // 8) * 8
    c_all = jnp.concatenate([c, c_ctx[None, :], jnp.zeros((n_rows - b - 1, d), F32)], axis=0)
    mod = _modulation(c_all, w_mod[0], b_mod[0][None, :])
    shift, scale, gate = mod[:, :d], mod[:, d:2 * d], mod[:, 2 * d:]
    nw_pre = norm_pre_w[0][None, :]

    x2 = x.reshape(b * seq, d)
    proj_l, dt_l = _inproj(x2, shift[:b, None, :], scale[:b, None, :], nw_pre, w_main, wdt, seq, N_MAIN, "inproj_latent")
    proj_c, dt_c = _inproj(ctx.reshape(b * ctx_len, d), shift[b:b + 1, None, :], scale[b:b + 1, None, :], nw_pre,
                           w_main, wdt, ctx_len, N_CTX, "inproj_ctx")
    proj_l = proj_l.reshape(b, seq, N_MAIN)
    proj_c = proj_c.reshape(b, ctx_len, N_CTX)
    dt_l = dt_l.reshape(b, seq, DT_LANES)
    dt_c = dt_c.reshape(b, ctx_len, DT_LANES)

    cos_t, sin_t = _rope_tables(seq)
    yr = _retention(proj_l, proj_c, cos_t, sin_t, dec, ret_gn_w[0][None, :])
    ys = _ssd(proj_l, proj_c, dt_l, dt_c, ssd_conv_w[0], ssd_conv_b[0][None, :], bias, alog, dsk, ssd_norm_w[0][None, :])

    out = _merge(yr.reshape(b * seq, RET_V), ys.reshape(b * seq, SSD_INNER), proj_l.reshape(b * seq, N_MAIN), x2,
                 gate[:b, None, :], w_ret_o[0].astype(BF16), w_ssd_o[0].astype(BF16), w_out[0].astype(BF16),
                 norm_post_w[0][None, :], seq)
    return out.reshape(b, seq, d)
```

```python
import jax
import jax.numpy as jnp
from jax import lax
from jax.experimental import pallas as pl
from jax.experimental.pallas import tpu as pltpu

F32 = jnp.float32
BF16 = jnp.bfloat16
HIGHEST = lax.Precision.HIGHEST

D_MODEL = 1024
EPS = 1e-6
GRID_W = 64
ROPE_THETA = 10000.0
ROPE_FREQS = 64

RET_HEADS = 4
RET_DK = 256
RET_DV = 512
RET_QK = RET_HEADS * RET_DK
RET_V = RET_HEADS * RET_DV
RET_CHUNK = 256

SSD_INNER = 2048
SSD_HEADDIM = 64
SSD_HEADS = 32
SSD_GROUPS = 8
SSD_HPG = 4
SSD_STATE = 128
SSD_BC = SSD_GROUPS * SSD_STATE
SSD_CONV_W = 5
SSD_GW = SSD_HPG * SSD_HEADDIM
SSD_CHUNK = 128
CONV_ROWS = 256
CONV_HALO = 16
DT_LANES = 128
SSD_EXP = 2 * SSD_GW
SSD_TAB = SSD_EXP + 128
NEG_BIG = -1e30

IN_SIZES = (RET_QK, RET_QK, RET_V, RET_V, SSD_INNER, SSD_INNER + 2 * SSD_BC, 2 * SSD_HEADS, 2 * D_MODEL)

COL_Q = 0
COL_K = COL_Q + RET_QK
COL_V = COL_K + RET_QK
COL_XS = COL_V + RET_V
COL_B = COL_XS + SSD_INNER
COL_C = COL_B + SSD_BC
N_CTX = COL_C + SSD_BC
COL_G = N_CTX
COL_Z = COL_G + RET_V
COL_GATES = COL_Z + SSD_INNER
N_MAIN = COL_GATES + 2 * D_MODEL

VMEM_LIMIT = 48 * 1024 * 1024

NT_DIMS = (((1,), (1,)), ((), ()))
TN_DIMS = (((0,), (0,)), ((), ()))


def _silu(x):
    return x * jax.nn.sigmoid(x)


def _softplus(x):
    return jnp.maximum(x, 0.0) + jnp.log1p(jnp.exp(-jnp.abs(x)))


def _log_sigmoid(x):
    return jnp.minimum(x, 0.0) - jnp.log1p(jnp.exp(-jnp.abs(x)))


def _split_bf16(a, parts):
    out = []
    for _ in range(parts):
        p = a.astype(BF16).astype(F32)
        out.append(p)
        a = a - p
    return out


def _mod_kernel(c_ref, w_ref, b_ref, o_ref):
    o_ref[...] = jnp.dot(_silu(c_ref[...]), w_ref[...], preferred_element_type=F32, precision=HIGHEST) + b_ref[...]


def _modulation(c_all, w_mod, b_mod):
    rows = c_all.shape[0]
    n = w_mod.shape[1]
    tn = D_MODEL
    return pl.pallas_call(
        _mod_kernel,
        out_shape=jax.ShapeDtypeStruct((rows, n), F32),
        grid=(n // tn,),
        in_specs=[pl.BlockSpec((rows, D_MODEL), lambda j: (0, 0)),
                  pl.BlockSpec((D_MODEL, tn), lambda j: (0, j)),
                  pl.BlockSpec((1, tn), lambda j: (0, j))],
        out_specs=pl.BlockSpec((rows, tn), lambda j: (0, j)),
        compiler_params=pltpu.CompilerParams(dimension_semantics=("arbitrary",), vmem_limit_bytes=VMEM_LIMIT),
        name="mod",
    )(c_all, w_mod, b_mod)


def _inproj_kernel(x_ref, shift_ref, scale_ref, nw_ref, w_ref, wdt_ref, o_ref, dt_ref, h_ref):
    @pl.when(pl.program_id(1) == 0)
    def _():
        x = x_ref[...]
        r = lax.rsqrt(jnp.mean(x * x, axis=-1, keepdims=True) + EPS)
        h = (x * r * nw_ref[...] * (1.0 + scale_ref[...]) + shift_ref[...]).astype(BF16)
        h_ref[...] = h
        dt_ref[...] = jnp.dot(h, wdt_ref[...], preferred_element_type=F32)

    o_ref[...] = jnp.dot(h_ref[...], w_ref[...], preferred_element_type=F32).astype(o_ref.dtype)


def _inproj(x2, shift, scale, nw, w, wdt, seq_len, n_cols, name):
    m = x2.shape[0]
    tm = min(1024, seq_len)
    tn = 1024
    per = seq_len // tm
    nb = shift.shape[0]
    mod_idx = (lambda i, j: (i // per, 0, 0)) if nb > 1 else (lambda i, j: (0, 0, 0))
    return pl.pallas_call(
        _inproj_kernel,
        out_shape=(jax.ShapeDtypeStruct((m, n_cols), BF16), jax.ShapeDtypeStruct((m, DT_LANES), F32)),
        grid=(m // tm, n_cols // tn),
        in_specs=[pl.BlockSpec((tm, D_MODEL), lambda i, j: (i, 0)),
                  pl.BlockSpec((None, 1, D_MODEL), mod_idx),
                  pl.BlockSpec((None, 1, D_MODEL), mod_idx),
                  pl.BlockSpec((1, D_MODEL), lambda i, j: (0, 0)),
                  pl.BlockSpec((D_MODEL, tn), lambda i, j: (0, j)),
                  pl.BlockSpec((D_MODEL, DT_LANES), lambda i, j: (0, 0))],
        out_specs=(pl.BlockSpec((tm, tn), lambda i, j: (i, j)),
                   pl.BlockSpec((tm, DT_LANES), lambda i, j: (i, 0))),
        scratch_shapes=[pltpu.VMEM((tm, D_MODEL), BF16)],
        compiler_params=pltpu.CompilerParams(dimension_semantics=("parallel", "arbitrary"), vmem_limit_bytes=VMEM_LIMIT),
        name=name,
    )(x2, shift, scale, nw, w, wdt)


def _ret_kernel(q_ref, k_ref, v_ref, g_ref, kc_ref, vc_ref, cos_ref, sin_ref, dec_ref, gnw_ref, o_ref,
                qs_ref, qin_ref, ks_ref, kend_ref, rhs_ref, sf_ref, sb_ref, dm_ref):
    seq = q_ref.shape[0]
    ch = RET_CHUNK
    n_ch = seq // ch
    dk = RET_DK

    lg = _log_sigmoid(dec_ref[...])
    lgf = lg[0:1, 0:1]
    lgb = lg[1:2, 0:1]

    ii = lax.broadcasted_iota(jnp.int32, (ch, ch), 0)
    jj = lax.broadcasted_iota(jnp.int32, (ch, ch), 1)
    dm_ref[...] = jnp.exp(jnp.abs(ii - jj).astype(F32) * jnp.where(ii >= jj, lgf, lgb))

    pos = lax.broadcasted_iota(jnp.int32, (ch, 1), 0).astype(F32)
    in_f = jnp.exp((pos + 1.0) * lgf)
    in_b = jnp.exp((ch - pos) * lgb)
    end_f = jnp.exp((ch - 1.0 - pos) * lgf)
    end_b = jnp.exp(pos * lgb)
    all_f = jnp.exp(ch * lgf)
    all_b = jnp.exp(ch * lgb)

    def rope(src, r0):
        x = src[pl.ds(r0, ch), :].astype(F32)
        parts = []
        for s in (0, 128):
            xh = x[:, s:s + 128]
            parts.append(xh * cos_ref[pl.ds(r0, ch), s:s + 128] + pltpu.roll(xh, 64, 1) * sin_ref[pl.ds(r0, ch), s:s + 128])
        return jnp.concatenate(parts, axis=1)

    def rope_rows(c, _):
        r0 = pl.multiple_of(c * ch, ch)
        qr = rope(q_ref, r0) * (RET_DK ** -0.5)
        qs_ref[pl.ds(r0, ch), :] = qr.astype(BF16)
        qin_ref[pl.ds(r0, ch), 0:dk] = (qr * in_f).astype(BF16)
        qin_ref[pl.ds(r0, ch), dk:2 * dk] = (qr * in_b).astype(BF16)
        kr = rope(k_ref, r0)
        ks_ref[pl.ds(r0, ch), :] = kr.astype(BF16)
        kend_ref[pl.ds(r0, ch), 0:dk] = (kr * end_f).astype(BF16)
        kend_ref[pl.ds(r0, ch), dk:2 * dk] = (kr * end_b).astype(BF16)
        return 0

    lax.fori_loop(0, n_ch, rope_rows, 0)

    def outer(kb, vb):
        return lax.dot_general(kb, vb, TN_DIMS, preferred_element_type=F32)

    kc = kc_ref[...].astype(F32)
    sf_ref[...] = outer((kc * end_f).astype(BF16), vc_ref[...])
    sb_ref[...] = outer((kc * end_b).astype(BF16), vc_ref[...])

    def states(t, _):
        cf = t
        cb = n_ch - 1 - t
        rf = pl.multiple_of(cf * ch, ch)
        rb = pl.multiple_of(cb * ch, ch)
        vf = v_ref[pl.ds(rf, ch), :]
        rhs_ref[cf, 0:ch, :] = vf
        rhs_ref[cf, ch:ch + dk, :] = sf_ref[...].astype(BF16)
        rhs_ref[cb, ch + dk:ch + 2 * dk, :] = sb_ref[...].astype(BF16)
        sf_ref[...] = sf_ref[...] * all_f + outer(kend_ref[pl.ds(rf, ch), 0:dk], vf)
        sb_ref[...] = sb_ref[...] * all_b + outer(kend_ref[pl.ds(rb, ch), dk:2 * dk], v_ref[pl.ds(rb, ch), :])
        return 0

    lax.fori_loop(0, n_ch, states, 0, unroll=2)

    def outputs(pair, _):
        cs_ = [2 * pair, 2 * pair + 1]
        starts = [pl.multiple_of(c * ch, ch) for c in cs_]
        scs = [lax.dot_general(qs_ref[pl.ds(r0, ch), :], ks_ref[pl.ds(r0, ch), :], NT_DIMS, preferred_element_type=F32)
               for r0 in starts]
        lhs = [jnp.concatenate([(sc * dm_ref[...]).astype(BF16), qin_ref[pl.ds(r0, ch), :]], axis=1)
               for sc, r0 in zip(scs, starts)]
        ys = [jnp.dot(l, rhs_ref[c], preferred_element_type=F32) for l, c in zip(lhs, cs_)]
        for r0, y in zip(starts, ys):
            mu = jnp.mean(y, axis=-1, keepdims=True)
            yc = y - mu
            var = jnp.mean(yc * yc, axis=-1, keepdims=True)
            g = g_ref[pl.ds(r0, ch), :].astype(F32)
            o_ref[pl.ds(r0, ch), :] = (yc * lax.rsqrt(var + EPS) * gnw_ref[...] * _silu(g)).astype(o_ref.dtype)
        return 0

    lax.fori_loop(0, n_ch // 2, outputs, 0)


def _retention(proj_l, proj_c, cos_t, sin_t, dec, gnw):
    b, seq, _ = proj_l.shape
    ctx_len = proj_c.shape[1]
    assert ctx_len == RET_CHUNK and seq % (2 * RET_CHUNK) == 0
    n_ch = seq // RET_CHUNK
    kq, kv = RET_DK, RET_DV
    const = dict(pipeline_mode=pl.Buffered(1))
    return pl.pallas_call(
        _ret_kernel,
        out_shape=jax.ShapeDtypeStruct((b, seq, RET_V), BF16),
        grid=(b, RET_HEADS),
        in_specs=[pl.BlockSpec((None, seq, kq), lambda i, h: (i, 0, COL_Q // kq + h)),
                  pl.BlockSpec((None, seq, kq), lambda i, h: (i, 0, COL_K // kq + h)),
                  pl.BlockSpec((None, seq, kv), lambda i, h: (i, 0, COL_V // kv + h)),
                  pl.BlockSpec((None, seq, kv), lambda i, h: (i, 0, COL_G // kv + h)),
                  pl.BlockSpec((None, ctx_len, kq), lambda i, h: (i, 0, COL_K // kq + h)),
                  pl.BlockSpec((None, ctx_len, kv), lambda i, h: (i, 0, COL_V // kv + h)),
                  pl.BlockSpec((seq, kq), lambda i, h: (0, 0), **const),
                  pl.BlockSpec((seq, kq), lambda i, h: (0, 0), **const),
                  pl.BlockSpec((None, 2, 128), lambda i, h: (h, 0, 0)),
                  pl.BlockSpec((1, kv), lambda i, h: (0, h))],
        out_specs=pl.BlockSpec((None, seq, kv), lambda i, h: (i, 0, h)),
        scratch_shapes=[pltpu.VMEM((seq, kq), BF16), pltpu.VMEM((seq, 2 * kq), BF16),
                        pltpu.VMEM((seq, kq), BF16), pltpu.VMEM((seq, 2 * kq), BF16),
                        pltpu.VMEM((n_ch, RET_CHUNK + 2 * kq, kv), BF16),
                        pltpu.VMEM((kq, kv), F32), pltpu.VMEM((kq, kv), F32),
                        pltpu.VMEM((RET_CHUNK, RET_CHUNK), F32)],
        compiler_params=pltpu.CompilerParams(dimension_semantics=("parallel", "arbitrary"), vmem_limit_bytes=VMEM_LIMIT),
        name="ret",
    )(proj_l, proj_l, proj_l, proj_l, proj_c, proj_c, cos_t, sin_t, dec, gnw)


def _ssd_kernel(xs_ref, b_ref, c_ref, z_ref, dt_ref, xsc_ref, bc_ref, dtc_ref,
                cwx_ref, cwb_ref, cwc_ref, cbx_ref, cbb_ref, cbc_ref, bias_ref, alog_ref, dsk_ref, nw_ref, o_ref,
                pad_ref, xc_ref, xm_ref, bs_ref, cs_ref, dtv_ref, cum_ref, upd_ref, dec_ref, sall_ref,
                padc_ref, xcc_ref, bsc_ref, dtvc_ref, updc_ref, decc_ref, st_ref, tri_ref, emat_ref):
    seq = xs_ref.shape[0]
    ctx_len = xsc_ref.shape[0]
    q = SSD_CHUNK
    n_q = seq // q
    n_qc = ctx_len // q
    gw = SSD_GW
    ns = SSD_STATE
    ex = SSD_EXP
    tab = SSD_TAB
    halo = CONV_HALO
    grp = pl.program_id(1)

    ti = lax.broadcasted_iota(jnp.int32, (q, 2 * q), 0)
    tj = lax.broadcasted_iota(jnp.int32, (q, 2 * q), 1)
    tri_ref[...] = jnp.where(jnp.where(tj >= q, tj - q, tj) <= ti, 1.0, 0.0).astype(BF16)
    er = lax.broadcasted_iota(jnp.int32, (2 * DT_LANES, tab), 0)
    ec = lax.broadcasted_iota(jnp.int32, (2 * DT_LANES, tab), 1)
    src_lane = jnp.where(er >= DT_LANES, er - DT_LANES, er) - 8 * grp
    want = jnp.where(ec < ex, 4 * (ec >> 8) + ((ec & (gw - 1)) >> 6), jnp.where(ec < ex + 64, (ec - ex) & 7, -1000))
    emat_ref[...] = jnp.where(src_lane == want, 1.0, 0.0).astype(BF16)

    lane_t = lax.broadcasted_iota(jnp.int32, (q, tab), 1)
    fwd_lane = (lane_t < gw) | ((lane_t >= ex) & (((lane_t - ex) & 7) < SSD_HPG))
    lane_c = lax.broadcasted_iota(jnp.int32, (q, 128), 1)
    piece = lane_c >> 3
    head = lane_c & 7

    bias = bias_ref[...]
    neg_a = -jnp.exp(alog_ref[...])
    lane_h = lax.broadcasted_iota(jnp.int32, (q, 128), 1)

    def prep(pad, srcs, weights, biases, dts, nrows, dsts, dtv_dst, with_xm):
        width = pad.shape[1]
        pad[0:halo, :] = jnp.zeros((halo, width), BF16)
        pad[nrows + halo:nrows + 2 * halo, :] = jnp.zeros((halo, width), BF16)
        off = 0
        for s in srcs:
            w_ = s.shape[1]
            pad[halo:nrows + halo, off:off + w_] = s[...]
            off += w_
        cw = jnp.concatenate([w[...] for w in weights], axis=1)
        cb = jnp.concatenate([b_[...] for b_ in biases], axis=1)
        rows = CONV_ROWS
        base = halo - SSD_CONV_W // 2
        strip_dst = []
        for dst in dsts:
            strip_dst += [(dst, o) for o in range(0, dst.shape[1], 128)]

        def conv(i, _):
            r0 = pl.multiple_of(i * rows, rows)
            for s_, (dst, o) in enumerate(strip_dst):
                lo_, hi_ = s_ * 128, (s_ + 1) * 128
                v = pad[pl.ds(r0, rows + 2 * halo), lo_:hi_].astype(F32)
                acc = cb[:, lo_:hi_] + cw[0:1, lo_:hi_] * v[base:base + rows, :]
                for k in range(1, SSD_CONV_W):
                    acc = acc + cw[k:k + 1, lo_:hi_] * v[base + k:base + k + rows, :]
                y = _silu(acc)
                dst[pl.ds(r0, rows), o:o + 128] = y.astype(dst.dtype)
                if with_xm and dst is xc_ref:
                    for half in range(rows // q):
                        yh = y[half * q:(half + 1) * q, :]
                        for r in range(SSD_HPG):
                            row = pl.multiple_of(i * (rows // q) * SSD_HPG * q, q) + (half * SSD_HPG + r) * q
                            if r // 2 == s_:
                                keep = (lane_h < SSD_HEADDIM) if r % 2 == 0 else (lane_h >= SSD_HEADDIM)
                                xm_ref[pl.ds(row, q), o:o + 128] = jnp.where(keep, yh, 0.0).astype(BF16)
                            else:
                                xm_ref[pl.ds(row, q), o:o + 128] = jnp.zeros((q, 128), BF16)
            hi, lo = _split_bf16(_softplus(dts[pl.ds(r0, rows), :] + bias), 2)
            dtv_dst[pl.ds(r0, rows), :] = jnp.dot(jnp.concatenate([hi, lo], axis=1).astype(BF16), emat_ref[...],
                                                  preferred_element_type=F32)
            return 0

        lax.fori_loop(0, nrows // rows, conv, 0)

    prep(pad_ref, (xs_ref, b_ref, c_ref), (cwx_ref, cwb_ref, cwc_ref), (cbx_ref, cbb_ref, cbc_ref),
         dt_ref, seq, (xc_ref, bs_ref, cs_ref), dtv_ref, True)
    prep(padc_ref, (xsc_ref, bc_ref), (cwx_ref, cwb_ref), (cbx_ref, cbb_ref),
         dtc_ref, ctx_len, (xcc_ref, bsc_ref), dtvc_ref, False)

    def tables(dtvs):
        a_s = [dtv * neg_a for dtv in dtvs]
        his, los = zip(*[_split_bf16(a, 2) for a in a_s])
        rhs = jnp.concatenate([jnp.concatenate(his, axis=1), jnp.concatenate(los, axis=1)], axis=0).astype(BF16)
        p_all = jnp.dot(tri_ref[...], rhs, preferred_element_type=F32)
        out = []
        for n, a in enumerate(a_s):
            p = p_all[:, n * tab:(n + 1) * tab]
            tot = p[q - 1:q, :]
            out.append((jnp.where(fwd_lane, p, tot - p + a), tot))
        return out

    def weighted_x(xcf, dtv, cum, tot):
        coef = dtv[:, 0:ex] * jnp.exp(tot[:, 0:ex] - cum[:, 0:ex])
        return (jnp.concatenate([xcf, xcf], axis=1) * coef).astype(BF16)

    def contribution(bb, xw):
        return lax.dot_general(bb, xw, TN_DIMS, preferred_element_type=F32)

    def pass_a_pair(starts, load, cum_dst, upd_dst, dec_dst, chunk_ids):
        xcs, bbs, dtvs = zip(*[load(r0) for r0 in starts])
        tabs = tables(dtvs)
        xws = [weighted_x(xcf, dtv, cum, tot) for xcf, dtv, (cum, tot) in zip(xcs, dtvs, tabs)]
        for n, (r0, c) in enumerate(zip(starts, chunk_ids)):
            cum, tot = tabs[n]
            if cum_dst is not None:
                cum_dst[pl.ds(r0, q), :] = cum
            upd_dst[c] = contribution(bbs[n], xws[n])
            d0 = c * 8 if isinstance(c, int) else pl.multiple_of(c * 8, 8)
            dec_dst[pl.ds(d0, 8), :] = jnp.broadcast_to(jnp.exp(tot[:, 0:ex]), (8, ex))

    assert n_qc == 2 and n_q % 2 == 0
    pass_a_pair([0, q], lambda r0: (xcc_ref[r0:r0 + q, :], bsc_ref[r0:r0 + q, :], dtvc_ref[r0:r0 + q, :]),
                None, updc_ref, decc_ref, [0, 1])

    def pass_a(pair, _):
        starts = [pl.multiple_of((2 * pair + n) * q, q) for n in range(2)]
        pass_a_pair(starts, lambda r0: (xc_ref[pl.ds(r0, q), :], bs_ref[pl.ds(r0, q), :], dtv_ref[pl.ds(r0, q), :]),
                    cum_ref, upd_ref, dec_ref, [2 * pair, 2 * pair + 1])
        return 0

    lax.fori_loop(0, n_q // 2, pass_a, 0)

    st_ref[...] = jnp.zeros((ns, ex), F32)
    for cc in range(n_qc):
        st_ref[:, 0:gw] = st_ref[:, 0:gw] * decc_ref[cc * 8:cc * 8 + 1, 0:gw] + updc_ref[cc, :, 0:gw]
    for cc in reversed(range(n_qc)):
        st_ref[:, gw:ex] = st_ref[:, gw:ex] * decc_ref[cc * 8:cc * 8 + 1, gw:ex] + updc_ref[cc, :, gw:ex]

    def pass_b(t, _):
        cf = t
        cb = n_q - 1 - t
        sall_ref[cf, :, 0:gw] = st_ref[:, 0:gw].astype(BF16)
        sall_ref[cb, :, gw:ex] = st_ref[:, gw:ex].astype(BF16)
        df = dec_ref[pl.ds(pl.multiple_of(cf * 8, 8), 8), :]
        db = dec_ref[pl.ds(pl.multiple_of(cb * 8, 8), 8), :]
        st_ref[:, 0:gw] = st_ref[:, 0:gw] * df[0:1, 0:gw] + upd_ref[cf, :, 0:gw]
        st_ref[:, gw:ex] = st_ref[:, gw:ex] * db[0:1, gw:ex] + upd_ref[cb, :, gw:ex]
        return 0

    lax.fori_loop(0, n_q, pass_b, 0)

    ii = lax.broadcasted_iota(jnp.int32, (q, q), 0)
    jj = lax.broadcasted_iota(jnp.int32, (q, q), 1)
    dsk = dsk_ref[...]
    nw = nw_ref[...]

    def exponents(r0):
        cum_c = cum_ref[pl.ds(r0, q), ex:tab]
        dt_c = dtv_ref[pl.ds(r0, q), ex:tab]
        ldt = jnp.where(dt_c > 0.0, jnp.log(dt_c), NEG_BIG)
        c0, c1, c2 = _split_bf16(cum_c, 3)
        l0, l1 = _split_bf16(ldt, 2)
        u = jnp.where(piece == 0, c0, jnp.where(piece == 1, c1, jnp.where(piece == 2, c2, jnp.where(piece < 8, 1.0, 0.0))))
        v = jnp.where(piece < 3, 1.0, jnp.where(piece == 3, -c0, jnp.where(piece == 4, -c1, jnp.where(
            piece == 5, -c2, jnp.where(piece == 6, l0, jnp.where(piece == 7, l1, 0.0))))))
        vst = jnp.concatenate([jnp.where(head == m, v, 0.0) for m in range(2 * SSD_HPG)], axis=0).astype(BF16)
        return lax.dot_general(u.astype(BF16), vst, NT_DIMS, preferred_element_type=F32)

    def intra_weights(s, arg_all):
        lower = ii >= jj
        ws = []
        for r in range(SSD_HPG):
            arg = jnp.where(lower, arg_all[:, r * q:(r + 1) * q], arg_all[:, (SSD_HPG + r) * q:(SSD_HPG + r + 1) * q])
            ws.append((s * jnp.exp(arg)).astype(BF16))
        return jnp.concatenate(ws, axis=1)

    def finish(r0, y, yi):
        ecum = jnp.exp(cum_ref[pl.ds(r0, q), 0:ex])
        y = y + yi[:, 0:gw] * ecum[:, 0:gw] + yi[:, gw:ex] * ecum[:, gw:ex]
        y = (y + dsk * xc_ref[pl.ds(r0, q), :]) * _silu(z_ref[pl.ds(r0, q), :].astype(F32))
        y = y * lax.rsqrt(jnp.mean(y * y, axis=-1, keepdims=True) + EPS) * nw
        o_ref[pl.ds(r0, q), :] = y.astype(o_ref.dtype)

    def pass_c(pair, _):
        cs_ = [2 * pair, 2 * pair + 1]
        starts = [pl.multiple_of(c * q, q) for c in cs_]
        args = [exponents(r0) for r0 in starts]
        cbs = [cs_ref[pl.ds(r0, q), :] for r0 in starts]
        scores = [lax.dot_general(cb, bs_ref[pl.ds(r0, q), :], NT_DIMS, preferred_element_type=F32)
                  for cb, r0 in zip(cbs, starts)]
        yis = [jnp.dot(cb, sall_ref[c], preferred_element_type=F32) for cb, c in zip(cbs, cs_)]
        wcat = [intra_weights(s, a) for s, a in zip(scores, args)]
        ys = [jnp.dot(w, xm_ref[pl.ds(pl.multiple_of(c * SSD_HPG * q, q), SSD_HPG * q), :], preferred_element_type=F32)
              for w, c in zip(wcat, cs_)]
        for r0, y, yi in zip(starts, ys, yis):
            finish(r0, y, yi)
        return 0

    lax.fori_loop(0, n_q // 2, pass_c, 0)


def _ssd(proj_l, proj_c, dt_l, dt_c, conv_w, conv_b, bias, alog, dsk, nw):
    b, seq, _ = proj_l.shape
    ctx_len = proj_c.shape[1]
    assert seq % CONV_ROWS == 0 and ctx_len % CONV_ROWS == 0 and CONV_ROWS % SSD_CHUNK == 0
    gw, ns, q, ex, tab = SSD_GW, SSD_STATE, SSD_CHUNK, SSD_EXP, SSD_TAB
    n_q, n_qc = seq // q, ctx_len // q
    cx, cb_, cc_ = 0, SSD_INNER // ns, (SSD_INNER + SSD_BC) // ns
    return pl.pallas_call(
        _ssd_kernel,
        out_shape=jax.ShapeDtypeStruct((b, seq, SSD_INNER), BF16),
        grid=(b, SSD_GROUPS),
        in_specs=[pl.BlockSpec((None, seq, gw), lambda i, g: (i, 0, COL_XS // gw + g)),
                  pl.BlockSpec((None, seq, ns), lambda i, g: (i, 0, COL_B // ns + g)),
                  pl.BlockSpec((None, seq, ns), lambda i, g: (i, 0, COL_C // ns + g)),
                  pl.BlockSpec((None, seq, gw), lambda i, g: (i, 0, COL_Z // gw + g)),
                  pl.BlockSpec((None, seq, DT_LANES), lambda i, g: (i, 0, 0)),
                  pl.BlockSpec((None, ctx_len, gw), lambda i, g: (i, 0, COL_XS // gw + g)),
                  pl.BlockSpec((None, ctx_len, ns), lambda i, g: (i, 0, COL_B // ns + g)),
                  pl.BlockSpec((None, ctx_len, DT_LANES), lambda i, g: (i, 0, 0)),
                  pl.BlockSpec((SSD_CONV_W, gw), lambda i, g: (0, cx + g)),
                  pl.BlockSpec((SSD_CONV_W, ns), lambda i, g: (0, cb_ + g)),
                  pl.BlockSpec((SSD_CONV_W, ns), lambda i, g: (0, cc_ + g)),
                  pl.BlockSpec((1, gw), lambda i, g: (0, cx + g)),
                  pl.BlockSpec((1, ns), lambda i, g: (0, cb_ + g)),
                  pl.BlockSpec((1, ns), lambda i, g: (0, cc_ + g)),
                  pl.BlockSpec((1, DT_LANES), lambda i, g: (0, 0)),
                  pl.BlockSpec((None, 1, tab), lambda i, g: (g, 0, 0)),
                  pl.BlockSpec((1, gw), lambda i, g: (0, g)),
                  pl.BlockSpec((1, gw), lambda i, g: (0, g))],
        out_specs=pl.BlockSpec((None, seq, gw), lambda i, g: (i, 0, g)),
        scratch_shapes=[pltpu.VMEM((seq + 2 * CONV_HALO, gw + 2 * ns), BF16),
                        pltpu.VMEM((seq, gw), F32),
                        pltpu.VMEM((n_q * SSD_HPG * q, gw), BF16),
                        pltpu.VMEM((seq, ns), BF16), pltpu.VMEM((seq, ns), BF16),
                        pltpu.VMEM((seq, tab), F32), pltpu.VMEM((seq, tab), F32),
                        pltpu.VMEM((n_q, ns, ex), F32), pltpu.VMEM((n_q * 8, ex), F32),
                        pltpu.VMEM((n_q, ns, ex), BF16),
                        pltpu.VMEM((ctx_len + 2 * CONV_HALO, gw + ns), BF16),
                        pltpu.VMEM((ctx_len, gw), F32), pltpu.VMEM((ctx_len, ns), BF16),
                        pltpu.VMEM((ctx_len, tab), F32),
                        pltpu.VMEM((n_qc, ns, ex), F32), pltpu.VMEM((n_qc * 8, ex), F32),
                        pltpu.VMEM((ns, ex), F32),
                        pltpu.VMEM((q, 2 * q), BF16), pltpu.VMEM((2 * DT_LANES, tab), BF16)],
        compiler_params=pltpu.CompilerParams(dimension_semantics=("parallel", "arbitrary"), vmem_limit_bytes=VMEM_LIMIT),
        name="ssd",
    )(proj_l, proj_l, proj_l, proj_l, dt_l, proj_c, proj_c, dt_c,
      conv_w, conv_w, conv_w, conv_b, conv_b, conv_b, bias, alog, dsk, nw)


def _merge_kernel(yr_ref, ys_ref, gt_ref, x_ref, gate_ref, wr_ref, ws_ref, wo_ref, nw_ref, o_ref):
    o_r = jnp.dot(yr_ref[...], wr_ref[...], preferred_element_type=F32)
    o_s = jnp.dot(ys_ref[...], ws_ref[...], preferred_element_type=F32)
    gt = gt_ref[...].astype(F32)
    m = jax.nn.sigmoid(gt[:, :D_MODEL]) * o_r + jax.nn.sigmoid(gt[:, D_MODEL:]) * o_s
    out = jnp.dot(m.astype(BF16), wo_ref[...], preferred_element_type=F32)
    nrm = out * lax.rsqrt(jnp.mean(out * out, axis=-1, keepdims=True) + EPS) * nw_ref[...]
    o_ref[...] = x_ref[...] + gate_ref[...] * nrm


def _merge(yr, ys, proj_l, x2, gate, w_ret_o, w_ssd_o, w_out, nw, seq_len):
    m = x2.shape[0]
    tm = min(512, seq_len)
    per = seq_len // tm
    const = dict(pipeline_mode=pl.Buffered(1))
    return pl.pallas_call(
        _merge_kernel,
        out_shape=jax.ShapeDtypeStruct((m, D_MODEL), F32),
        grid=(m // tm,),
        in_specs=[pl.BlockSpec((tm, RET_V), lambda i: (i, 0)),
                  pl.BlockSpec((tm, SSD_INNER), lambda i: (i, 0)),
                  pl.BlockSpec((tm, 2 * D_MODEL), lambda i: (i, COL_GATES // (2 * D_MODEL))),
                  pl.BlockSpec((tm, D_MODEL), lambda i: (i, 0)),
                  pl.BlockSpec((None, 1, D_MODEL), lambda i: (i // per, 0, 0)),
                  pl.BlockSpec((RET_V, D_MODEL), lambda i: (0, 0), **const),
                  pl.BlockSpec((SSD_INNER, D_MODEL), lambda i: (0, 0), **const),
                  pl.BlockSpec((D_MODEL, D_MODEL), lambda i: (0, 0), **const),
                  pl.BlockSpec((1, D_MODEL), lambda i: (0, 0))],
        out_specs=pl.BlockSpec((tm, D_MODEL), lambda i: (i, 0)),
        compiler_params=pltpu.CompilerParams(dimension_semantics=("parallel",), vmem_limit_bytes=VMEM_LIMIT),
        name="merge",
    )(yr, ys, proj_l, x2, gate, w_ret_o, w_ssd_o, w_out, nw)


def _rope_tables(n_tokens):
    pos = jnp.arange(n_tokens)
    row = (pos // GRID_W).astype(F32)
    col = (pos % GRID_W).astype(F32)
    inv_freq = ROPE_THETA ** (-jnp.arange(ROPE_FREQS, dtype=F32) / ROPE_FREQS)
    ar = row[:, None] * inv_freq
    ac = col[:, None] * inv_freq
    cos_t = jnp.concatenate([jnp.cos(ar), jnp.cos(ar), jnp.cos(ac), jnp.cos(ac)], axis=1)
    sin_t = jnp.concatenate([-jnp.sin(ar), jnp.sin(ar), -jnp.sin(ac), jnp.sin(ac)], axis=1)
    return cos_t, sin_t


def _group_major(t):
    lead = t.shape[:-2]
    t = t.reshape(lead + (2, SSD_GROUPS, SSD_HPG))
    t = jnp.moveaxis(t, -3, -2)
    return t.reshape(lead + (SSD_GROUPS, 2 * SSD_HPG))


def _table_lanes(t):
    gm = _group_major(t)
    expanded = jnp.repeat(gm, SSD_HEADDIM, axis=1)
    compact = jnp.pad(jnp.tile(gm, (1, 8)), ((0, 0), (0, 64)))
    return jnp.concatenate([expanded, compact], axis=1)[:, None, :]


def kernel(x, c, ctx, c_ctx, w_mod, b_mod, norm_pre_w, norm_post_w, w_in, ret_decay, ret_gn_w, ssd_conv_w, ssd_conv_b,
           ssd_dt_bias, ssd_a_log, ssd_D, ssd_norm_w, w_ret_o, w_ssd_o, w_out):
    assert w_in.shape[0] == 1, "single-layer problem: the context stream is never updated"
    b, seq, d = x.shape
    ctx_len = ctx.shape[1]

    offs = [0]
    for s in IN_SIZES:
        offs.append(offs[-1] + s)
    wq, wk, wv, wg, wz, wxbc, wdt, wgt = [w_in[0][:, offs[i]:offs[i + 1]] for i in range(len(IN_SIZES))]
    w_main = jnp.concatenate([wq, wk, wv, wxbc, wg, wz, wgt], axis=1).astype(BF16)
    wdt = _group_major(wdt.reshape(d, 2, SSD_HEADS)).reshape(d, 2 * SSD_HEADS)
    wdt = jnp.pad(wdt, ((0, 0), (0, DT_LANES - 2 * SSD_HEADS))).astype(BF16)
    bias = jnp.pad(_group_major(ssd_dt_bias[0]).reshape(1, 2 * SSD_HEADS), ((0, 0), (0, DT_LANES - 2 * SSD_HEADS)))
    alog = _table_lanes(ssd_a_log[0])
    dsk = jnp.repeat(ssd_D[0], SSD_HEADDIM)[None, :]
    dec = jnp.broadcast_to(ret_decay[0].T[:, :, None], (RET_HEADS, 2, 128))

    n_rows = -(-(b + 1) // 8) * 8
    c_all = jnp.concatenate([c, c_ctx[None, :], jnp.zeros((n_rows - b - 1, d), F32)], axis=0)
    mod = _modulation(c_all, w_mod[0], b_mod[0][None, :])
    shift, scale, gate = mod[:, :d], mod[:, d:2 * d], mod[:, 2 * d:]
    nw_pre = norm_pre_w[0][None, :]

    x2 = x.reshape(b * seq, d)
    proj_l, dt_l = _inproj(x2, shift[:b, None, :], scale[:b, None, :], nw_pre, w_main, wdt, seq, N_MAIN, "inproj_latent")
    proj_c, dt_c = _inproj(ctx.reshape(b * ctx_len, d), shift[b:b + 1, None, :], scale[b:b + 1, None, :], nw_pre,
                           w_main, wdt, ctx_len, N_CTX, "inproj_ctx")
    proj_l = proj_l.reshape(b, seq, N_MAIN)
    proj_c = proj_c.reshape(b, ctx_len, N_CTX)
    dt_l = dt_l.reshape(b, seq, DT_LANES)
    dt_c = dt_c.reshape(b, ctx_len, DT_LANES)

    cos_t, sin_t = _rope_tables(seq)
    yr = _retention(proj_l, proj_c, cos_t, sin_t, dec, ret_gn_w[0][None, :])
    ys = _ssd(proj_l, proj_c, dt_l, dt_c, ssd_conv_w[0], ssd_conv_b[0][None, :], bias, alog, dsk, ssd_norm_w[0][None, :])

    out = _merge(yr.reshape(b * seq, RET_V), ys.reshape(b * seq, SSD_INNER), proj_l.reshape(b * seq, N_MAIN), x2,
                 gate[:b, None, :], w_ret_o[0].astype(BF16), w_ssd_o[0].astype(BF16), w_out[0].astype(BF16),
                 norm_post_w[0][None, :], seq)
    return out.reshape(b, seq, d)
```

```python
import math

import jax
import jax.numpy as jnp
import numpy as np
from jax import lax
from jax.experimental import pallas as pl
from jax.experimental.pallas import tpu as pltpu

F32 = jnp.float32
BF16 = jnp.bfloat16
HIGHEST = lax.Precision.HIGHEST

D_MODEL = 1024
EPS = 1e-6
GRID_W = 64
ROPE_THETA = 10000.0
ROPE_FREQS = 64

RET_HEADS = 4
RET_DK = 256
RET_DV = 512
RET_QK = RET_HEADS * RET_DK
RET_V = RET_HEADS * RET_DV
RET_CHUNK = 256
RET_GROUP = 4

SSD_INNER = 2048
SSD_HEADDIM = 64
SSD_HEADS = 32
SSD_GROUPS = 8
SSD_HPG = 4
SSD_STATE = 128
SSD_BC = SSD_GROUPS * SSD_STATE
SSD_CONV_W = 5
SSD_GW = SSD_HPG * SSD_HEADDIM
SSD_CHUNK = 128
SSD_GROUP_A = 4
SSD_GROUP_C = 4
CONV_ROWS = 256
CONV_HALO = 16
CONV_SHIFTS = tuple(o for o in range(-(SSD_CONV_W // 2), SSD_CONV_W // 2 + 1) if o != 0)
DT_LANES = 128
SSD_EXP = 2 * SSD_GW
SSD_TAB = SSD_EXP + 128
NEG_BIG = -1e30
LOG2_E = 1.4426950408889634

IN_SIZES = (RET_QK, RET_QK, RET_V, RET_V, SSD_INNER, SSD_INNER + 2 * SSD_BC, 2 * SSD_HEADS, 2 * D_MODEL)

COL_Q = 0
COL_K = COL_Q + RET_QK
COL_V = COL_K + RET_QK
COL_G = COL_V + RET_V
COL_Z = COL_G + RET_V
COL_XS = COL_Z + SSD_INNER
COL_B = COL_XS + SSD_INNER
COL_C = COL_B + SSD_BC
COL_GATES = COL_C + SSD_BC
N_MAIN = COL_GATES + 2 * D_MODEL
CTX_K = 0
CTX_V = CTX_K + RET_QK
CTX_XS = CTX_V + RET_V
CTX_B = CTX_XS + SSD_INNER
CTX_C = CTX_B + SSD_BC
N_CTX = CTX_C + SSD_BC
INPROJ_TN = 1024
CTX_TILES_KV = (CTX_XS - CTX_K) // INPROJ_TN
CTX_OFF_KV = COL_K // INPROJ_TN
CTX_OFF_XBC = (COL_XS - CTX_XS) // INPROJ_TN

VMEM_LIMIT = 48 * 1024 * 1024

NT_DIMS = (((1,), (1,)), ((), ()))
TN_DIMS = (((0,), (0,)), ((), ()))


def _silu(x):
    return x * jax.nn.sigmoid(x)


def _softplus(x):
    return jnp.maximum(x, 0.0) + jnp.log1p(jnp.exp(-jnp.abs(x)))


def _log_sigmoid(x):
    return jnp.minimum(x, 0.0) - jnp.log1p(jnp.exp(-jnp.abs(x)))


def _split_bf16(a, parts):
    out = []
    for _ in range(parts):
        p = a.astype(BF16).astype(F32)
        out.append(p)
        a = a - p
    return out


def _mod_kernel(c_ref, w_ref, b_ref, o_ref):
    o_ref[...] = jnp.dot(_silu(c_ref[...]), w_ref[...], preferred_element_type=F32, precision=HIGHEST) + b_ref[...]


def _modulation(c_all, w_mod, b_mod):
    rows = c_all.shape[0]
    n = w_mod.shape[1]
    tn = D_MODEL
    return pl.pallas_call(
        _mod_kernel,
        out_shape=jax.ShapeDtypeStruct((rows, n), F32),
        grid=(n // tn,),
        in_specs=[pl.BlockSpec((rows, D_MODEL), lambda j: (0, 0)),
                  pl.BlockSpec((D_MODEL, tn), lambda j: (0, j)),
                  pl.BlockSpec((1, tn), lambda j: (0, j))],
        out_specs=pl.BlockSpec((rows, tn), lambda j: (0, j)),
        compiler_params=pltpu.CompilerParams(dimension_semantics=("arbitrary",), vmem_limit_bytes=VMEM_LIMIT),
        name="mod",
    )(c_all, w_mod, b_mod)


def _inproj_kernel(x_ref, shift_ref, scale_ref, nw_ref, w_ref, wdt_ref, o_ref, dt_ref, h_ref):
    @pl.when(pl.program_id(1) == 0)
    def _():
        x = x_ref[...]
        r = lax.rsqrt(jnp.mean(x * x, axis=-1, keepdims=True) + EPS)
        h = (x * r * nw_ref[...] * (1.0 + scale_ref[...]) + shift_ref[...]).astype(BF16)
        h_ref[...] = h
        dt_ref[...] = jnp.dot(h, wdt_ref[...], preferred_element_type=F32)

    o_ref[...] = jnp.dot(h_ref[...], w_ref[...], preferred_element_type=F32).astype(o_ref.dtype)


def _inproj(x2, shift, scale, nw, w, wdt, seq_len, n_cols, name):
    m = x2.shape[0]
    nb = shift.shape[0]
    tm = min(2048, seq_len if nb > 1 else m)
    tn = INPROJ_TN
    per = seq_len // tm
    mod_idx = (lambda i, j: (i // per, 0, 0)) if nb > 1 else (lambda i, j: (0, 0, 0))
    if n_cols == N_MAIN:
        w_idx = lambda i, j: (0, j)
    else:
        w_idx = lambda i, j: (0, jnp.where(j < CTX_TILES_KV, j + CTX_OFF_KV, j + CTX_OFF_XBC))
    return pl.pallas_call(
        _inproj_kernel,
        out_shape=(jax.ShapeDtypeStruct((m, n_cols), BF16), jax.ShapeDtypeStruct((m, DT_LANES), F32)),
        grid=(m // tm, n_cols // tn),
        in_specs=[pl.BlockSpec((tm, D_MODEL), lambda i, j: (i, 0)),
                  pl.BlockSpec((None, 1, D_MODEL), mod_idx),
                  pl.BlockSpec((None, 1, D_MODEL), mod_idx),
                  pl.BlockSpec((1, D_MODEL), lambda i, j: (0, 0)),
                  pl.BlockSpec((D_MODEL, tn), w_idx),
                  pl.BlockSpec((D_MODEL, DT_LANES), lambda i, j: (0, 0))],
        out_specs=(pl.BlockSpec((tm, tn), lambda i, j: (i, j)),
                   pl.BlockSpec((tm, DT_LANES), lambda i, j: (i, 0))),
        scratch_shapes=[pltpu.VMEM((tm, D_MODEL), BF16)],
        compiler_params=pltpu.CompilerParams(dimension_semantics=("parallel", "arbitrary"), vmem_limit_bytes=VMEM_LIMIT),
        name=name,
    )(x2, shift, scale, nw, w, wdt)


def _ret_kernel(q_ref, k_ref, v_ref, g_ref, kc_ref, vc_ref, cos_ref, sin_ref, dec_ref, gnw_ref, o_ref,
                qs_ref, qin_ref, ks_ref, kend_ref, rhs_ref, sf_ref, sb_ref, dm_ref, pd_ref):
    seq = q_ref.shape[0]
    ch = RET_CHUNK
    n_ch = seq // ch
    dk = RET_DK

    lg = _log_sigmoid(dec_ref[...])
    lgf = lg[0:1, 0:1]
    lgb = lg[1:2, 0:1]

    ii = lax.broadcasted_iota(jnp.int32, (ch, ch), 0)
    jj = lax.broadcasted_iota(jnp.int32, (ch, ch), 1)
    dm_ref[...] = jnp.exp(jnp.abs(ii - jj).astype(F32) * jnp.where(ii >= jj, lgf, lgb))

    pos = lax.broadcasted_iota(jnp.int32, (ch, 128), 0).astype(F32)
    pd_ref[0] = jnp.exp((pos + 1.0) * lgf)
    pd_ref[1] = jnp.exp((ch - pos) * lgb)
    pd_ref[2] = jnp.exp((ch - 1.0 - pos) * lgf)
    pd_ref[3] = jnp.exp(pos * lgb)
    all_f = jnp.exp(ch * lgf)
    all_b = jnp.exp(ch * lgb)

    def rope_rows(c, _):
        r0 = pl.multiple_of(c * ch, ch)
        for s in (0, 128):
            cos_h = cos_ref[pl.ds(r0, ch), s:s + 128]
            sin_h = sin_ref[pl.ds(r0, ch), s:s + 128]
            qh = q_ref[pl.ds(r0, ch), s:s + 128].astype(F32)
            qr = (qh * cos_h + pltpu.roll(qh, 64, 1) * sin_h) * (RET_DK ** -0.5)
            qs_ref[pl.ds(r0, ch), s:s + 128] = qr.astype(BF16)
            qin_ref[pl.ds(r0, ch), s:s + 128] = (qr * pd_ref[0]).astype(BF16)
            qin_ref[pl.ds(r0, ch), dk + s:dk + s + 128] = (qr * pd_ref[1]).astype(BF16)
            kh = k_ref[pl.ds(r0, ch), s:s + 128].astype(F32)
            kr = kh * cos_h + pltpu.roll(kh, 64, 1) * sin_h
            ks_ref[pl.ds(r0, ch), s:s + 128] = kr.astype(BF16)
            kend_ref[pl.ds(r0, ch), s:s + 128] = (kr * pd_ref[2]).astype(BF16)
            kend_ref[pl.ds(r0, ch), dk + s:dk + s + 128] = (kr * pd_ref[3]).astype(BF16)
        return 0

    lax.fori_loop(0, n_ch, rope_rows, 0)

    def outer(kb, vb):
        return lax.dot_general(kb, vb, TN_DIMS, preferred_element_type=F32)

    kc = kc_ref[...].astype(F32)
    end_f = jnp.concatenate([pd_ref[2], pd_ref[2]], axis=1)
    end_b = jnp.concatenate([pd_ref[3], pd_ref[3]], axis=1)
    sf_ref[...] = outer((kc * end_f).astype(BF16), vc_ref[...])
    sb_ref[...] = outer((kc * end_b).astype(BF16), vc_ref[...])

    def states(t, _):
        cf = t
        cb = n_ch - 1 - t
        rf = pl.multiple_of(cf * ch, ch)
        rb = pl.multiple_of(cb * ch, ch)
        vf = v_ref[pl.ds(rf, ch), :]
        rhs_ref[cf, 0:ch, :] = vf
        rhs_ref[cf, ch:ch + dk, :] = sf_ref[...].astype(BF16)
        rhs_ref[cb, ch + dk:ch + 2 * dk, :] = sb_ref[...].astype(BF16)
        sf_ref[...] = sf_ref[...] * all_f + outer(kend_ref[pl.ds(rf, ch), 0:dk], vf)
        sb_ref[...] = sb_ref[...] * all_b + outer(kend_ref[pl.ds(rb, ch), dk:2 * dk], v_ref[pl.ds(rb, ch), :])
        return 0

    lax.fori_loop(0, n_ch, states, 0, unroll=2)

    group = math.gcd(RET_GROUP, n_ch)

    def outputs(pair, _):
        cs_ = [group * pair + n for n in range(group)]
        starts = [pl.multiple_of(c * ch, ch) for c in cs_]
        scs = [lax.dot_general(qs_ref[pl.ds(r0, ch), :], ks_ref[pl.ds(r0, ch), :], NT_DIMS, preferred_element_type=F32)
               for r0 in starts]
        lhs = [jnp.concatenate([(sc * dm_ref[...]).astype(BF16), qin_ref[pl.ds(r0, ch), :]], axis=1)
               for sc, r0 in zip(scs, starts)]
        ys = [jnp.dot(l, rhs_ref[c], preferred_element_type=F32) for l, c in zip(lhs, cs_)]
        for r0, y in zip(starts, ys):
            mu = jnp.mean(y, axis=-1, keepdims=True)
            yc = y - mu
            var = jnp.mean(yc * yc, axis=-1, keepdims=True)
            g = g_ref[pl.ds(r0, ch), :].astype(F32)
            o_ref[pl.ds(r0, ch), :] = (yc * lax.rsqrt(var + EPS) * gnw_ref[...] * _silu(g)).astype(o_ref.dtype)
        return 0

    lax.fori_loop(0, n_ch // group, outputs, 0)


def _retention(proj_l, proj_c, cos_t, sin_t, dec, gnw):
    b, seq, _ = proj_l.shape
    ctx_len = proj_c.shape[1]
    assert ctx_len == RET_CHUNK and seq % (2 * RET_CHUNK) == 0
    n_ch = seq // RET_CHUNK
    kq, kv = RET_DK, RET_DV
    const = dict(pipeline_mode=pl.Buffered(1))
    return pl.pallas_call(
        _ret_kernel,
        out_shape=jax.ShapeDtypeStruct((b, seq, RET_V), BF16),
        grid=(b, RET_HEADS),
        in_specs=[pl.BlockSpec((None, seq, kq), lambda i, h: (i, 0, COL_Q // kq + h)),
                  pl.BlockSpec((None, seq, kq), lambda i, h: (i, 0, COL_K // kq + h)),
                  pl.BlockSpec((None, seq, kv), lambda i, h: (i, 0, COL_V // kv + h)),
                  pl.BlockSpec((None, seq, kv), lambda i, h: (i, 0, COL_G // kv + h)),
                  pl.BlockSpec((None, ctx_len, kq), lambda i, h: (i, 0, CTX_K // kq + h)),
                  pl.BlockSpec((None, ctx_len, kv), lambda i, h: (i, 0, CTX_V // kv + h)),
                  pl.BlockSpec((seq, kq), lambda i, h: (0, 0), **const),
                  pl.BlockSpec((seq, kq), lambda i, h: (0, 0), **const),
                  pl.BlockSpec((None, 2, 128), lambda i, h: (h, 0, 0)),
                  pl.BlockSpec((1, kv), lambda i, h: (0, h))],
        out_specs=pl.BlockSpec((None, seq, kv), lambda i, h: (i, 0, h)),
        scratch_shapes=[pltpu.VMEM((seq, kq), BF16), pltpu.VMEM((seq, 2 * kq), BF16),
                        pltpu.VMEM((seq, kq), BF16), pltpu.VMEM((seq, 2 * kq), BF16),
                        pltpu.VMEM((n_ch, RET_CHUNK + 2 * kq, kv), BF16),
                        pltpu.VMEM((kq, kv), F32), pltpu.VMEM((kq, kv), F32),
                        pltpu.VMEM((RET_CHUNK, RET_CHUNK), F32), pltpu.VMEM((4, RET_CHUNK, 128), F32)],
        compiler_params=pltpu.CompilerParams(dimension_semantics=("parallel", "arbitrary"), vmem_limit_bytes=VMEM_LIMIT),
        name="ret",
    )(proj_l, proj_l, proj_l, proj_l, proj_c, proj_c, cos_t, sin_t, dec, gnw)


def _ssd_kernel(xs_ref, b_ref, c_ref, z_ref, dt_ref, xsc_ref, bc_ref, dtc_ref,
                cwx_ref, cwb_ref, cwc_ref, cbx_ref, cbb_ref, cbc_ref, bias_ref, alog_ref, dsk_ref, nw_ref,
                shm_ref, tri_ref, emat_ref, o_ref,
                pad_ref, xc_ref, xm_ref, bs_ref, cs_ref, dtv_ref, cum_ref, upd_ref, dec_ref, sall_ref,
                padc_ref, xcc_ref, bsc_ref, dtvc_ref, updc_ref, decc_ref, st_ref):
    seq = xs_ref.shape[0]
    ctx_len = xsc_ref.shape[0]
    q = SSD_CHUNK
    n_q = seq // q
    n_qc = ctx_len // q
    gw = SSD_GW
    ns = SSD_STATE
    ex = SSD_EXP
    tab = SSD_TAB
    halo = CONV_HALO

    lane_t = lax.broadcasted_iota(jnp.int32, (q, tab), 1)
    fwd_lane = (lane_t < gw) | ((lane_t >= ex) & (((lane_t - ex) & 7) < SSD_HPG))
    lane_c = lax.broadcasted_iota(jnp.int32, (q, 128), 1)
    piece = lane_c >> 3
    head = lane_c & 7

    bias = bias_ref[...]
    neg_a = -jnp.exp(alog_ref[...]) * LOG2_E
    lane_h = lax.broadcasted_iota(jnp.int32, (q, 128), 1)

    def prep(pad, srcs, weights, biases, dts, nrows, dsts, dtv_dst, with_xm):
        width = pad.shape[1]
        pad[0:halo, :] = jnp.zeros((halo, width), BF16)
        pad[nrows + halo:nrows + 2 * halo, :] = jnp.zeros((halo, width), BF16)
        off = 0
        for s in srcs:
            w_ = s.shape[1]
            pad[halo:nrows + halo, off:off + w_] = s[...]
            off += w_
        cw = jnp.concatenate([w[...] for w in weights], axis=1)
        cb = jnp.concatenate([b_[...] for b_ in biases], axis=1)
        rows = CONV_ROWS
        half_w = SSD_CONV_W // 2
        strip_dst = []
        for dst in dsts:
            strip_dst += [(dst, o) for o in range(0, dst.shape[1], 128)]
        sub8 = lax.broadcasted_iota(jnp.int32, (8, 128), 0)

        def conv(i, _):
            r0 = pl.multiple_of(i * rows, rows)
            blk = pad[pl.ds(r0 + halo, rows), :]
            sh = jnp.dot(shm_ref[...], blk, preferred_element_type=F32)
            prev = pad[pl.ds(r0, halo), :].astype(F32)[halo - 8:halo, :]
            nxt = pad[pl.ds(r0 + halo + rows, halo), :].astype(F32)[0:8, :]
            for s_, (dst, o) in enumerate(strip_dst):
                lo_, hi_ = s_ * 128, (s_ + 1) * 128
                acc = cb[:, lo_:hi_] + cw[half_w:half_w + 1, lo_:hi_] * blk[:, lo_:hi_].astype(F32)
                for n, off_ in enumerate(CONV_SHIFTS):
                    shf = sh[n * rows:(n + 1) * rows, lo_:hi_]
                    if off_ < 0:
                        edge = shf[0:8, :]
                        for e in range(-off_):
                            edge = jnp.where(sub8 == e, prev[8 + e + off_:9 + e + off_, lo_:hi_], edge)
                        shf = jnp.concatenate([edge, shf[8:, :]], axis=0)
                    else:
                        edge = shf[rows - 8:rows, :]
                        for e in range(off_):
                            edge = jnp.where(sub8 == 7 - e, nxt[off_ - 1 - e:off_ - e, lo_:hi_], edge)
                        shf = jnp.concatenate([shf[:rows - 8, :], edge], axis=0)
                    acc = acc + cw[off_ + half_w:off_ + half_w + 1, lo_:hi_] * shf
                y = _silu(acc)
                dst[pl.ds(r0, rows), o:o + 128] = y.astype(dst.dtype)
                if with_xm and dst is xc_ref:
                    for half in range(rows // q):
                        yh = y[half * q:(half + 1) * q, :]
                        for r in range(SSD_HPG):
                            row = pl.multiple_of(i * (rows // q) * SSD_HPG * q, q) + (half * SSD_HPG + r) * q
                            if r // 2 == s_:
                                keep = (lane_h < SSD_HEADDIM) if r % 2 == 0 else (lane_h >= SSD_HEADDIM)
                                xm_ref[pl.ds(row, q), o:o + 128] = jnp.where(keep, yh, 0.0).astype(BF16)
                            else:
                                xm_ref[pl.ds(row, q), o:o + 128] = jnp.zeros((q, 128), BF16)
            hi, lo = _split_bf16(_softplus(dts[pl.ds(r0, rows), :] + bias), 2)
            dtv_dst[pl.ds(r0, rows), :] = jnp.dot(jnp.concatenate([hi, lo], axis=1).astype(BF16), emat_ref[...],
                                                  preferred_element_type=F32)
            return 0

        lax.fori_loop(0, nrows // rows, conv, 0)

    prep(pad_ref, (xs_ref, b_ref, c_ref), (cwx_ref, cwb_ref, cwc_ref), (cbx_ref, cbb_ref, cbc_ref),
         dt_ref, seq, (xc_ref, bs_ref, cs_ref), dtv_ref, True)
    prep(padc_ref, (xsc_ref, bc_ref), (cwx_ref, cwb_ref), (cbx_ref, cbb_ref),
         dtc_ref, ctx_len, (xcc_ref, bsc_ref), dtvc_ref, False)

    def tables(dtvs):
        a_s = [dtv * neg_a for dtv in dtvs]
        his, los = zip(*[_split_bf16(a, 2) for a in a_s])
        rhs = jnp.concatenate([jnp.concatenate(his, axis=1), jnp.concatenate(los, axis=1)], axis=0).astype(BF16)
        p_all = jnp.dot(tri_ref[...], rhs, preferred_element_type=F32)
        out = []
        for n, a in enumerate(a_s):
            p = p_all[:, n * tab:(n + 1) * tab]
            tot = p[q - 1:q, :]
            out.append((jnp.where(fwd_lane, p, tot - p + a), tot))
        return out

    def weighted_x(xcf, dtv, cum, tot):
        coef = dtv[:, 0:ex] * jnp.exp2(tot[:, 0:ex] - cum[:, 0:ex])
        return (jnp.concatenate([xcf, xcf], axis=1) * coef).astype(BF16)

    def contribution(bb, xw):
        return lax.dot_general(bb, xw, TN_DIMS, preferred_element_type=F32)

    def pass_a_pair(starts, load, cum_dst, upd_dst, dec_dst, chunk_ids):
        xcs, bbs, dtvs = zip(*[load(r0) for r0 in starts])
        tabs = tables(dtvs)
        xws = [weighted_x(xcf, dtv, cum, tot) for xcf, dtv, (cum, tot) in zip(xcs, dtvs, tabs)]
        for n, (r0, c) in enumerate(zip(starts, chunk_ids)):
            cum, tot = tabs[n]
            if cum_dst is not None:
                cum_dst[pl.ds(r0, q), :] = cum
            upd_dst[c] = contribution(bbs[n], xws[n])
            d0 = c * 8 if isinstance(c, int) else pl.multiple_of(c * 8, 8)
            dec_dst[pl.ds(d0, 8), :] = jnp.broadcast_to(jnp.exp2(tot[:, 0:ex]), (8, ex))

    assert n_qc == 2 and n_q % SSD_GROUP_A == 0 and n_q % SSD_GROUP_C == 0
    pass_a_pair([0, q], lambda r0: (xcc_ref[r0:r0 + q, :], bsc_ref[r0:r0 + q, :], dtvc_ref[r0:r0 + q, :]),
                None, updc_ref, decc_ref, [0, 1])

    def pass_a(grp_i, _):
        ids = [SSD_GROUP_A * grp_i + n for n in range(SSD_GROUP_A)]
        starts = [pl.multiple_of(c * q, q) for c in ids]
        pass_a_pair(starts, lambda r0: (xc_ref[pl.ds(r0, q), :], bs_ref[pl.ds(r0, q), :], dtv_ref[pl.ds(r0, q), :]),
                    cum_ref, upd_ref, dec_ref, ids)
        return 0

    lax.fori_loop(0, n_q // SSD_GROUP_A, pass_a, 0)

    st_ref[...] = jnp.zeros((ns, ex), F32)
    for cc in range(n_qc):
        st_ref[:, 0:gw] = st_ref[:, 0:gw] * decc_ref[cc * 8:cc * 8 + 1, 0:gw] + updc_ref[cc, :, 0:gw]
    for cc in reversed(range(n_qc)):
        st_ref[:, gw:ex] = st_ref[:, gw:ex] * decc_ref[cc * 8:cc * 8 + 1, gw:ex] + updc_ref[cc, :, gw:ex]

    def pass_b(t, _):
        cf = t
        cb = n_q - 1 - t
        sall_ref[cf, :, 0:gw] = st_ref[:, 0:gw].astype(BF16)
        sall_ref[cb, :, gw:ex] = st_ref[:, gw:ex].astype(BF16)
        df = dec_ref[pl.ds(pl.multiple_of(cf * 8, 8), 8), :]
        db = dec_ref[pl.ds(pl.multiple_of(cb * 8, 8), 8), :]
        st_ref[:, 0:gw] = st_ref[:, 0:gw] * df[0:1, 0:gw] + upd_ref[cf, :, 0:gw]
        st_ref[:, gw:ex] = st_ref[:, gw:ex] * db[0:1, gw:ex] + upd_ref[cb, :, gw:ex]
        return 0

    lax.fori_loop(0, n_q, pass_b, 0)

    ii = lax.broadcasted_iota(jnp.int32, (q, q), 0)
    jj = lax.broadcasted_iota(jnp.int32, (q, q), 1)
    dsk = dsk_ref[...]
    nw = nw_ref[...]

    def exponents(r0):
        cum_c = cum_ref[pl.ds(r0, q), ex:tab]
        dt_c = dtv_ref[pl.ds(r0, q), ex:tab]
        ldt = jnp.where(dt_c > 0.0, jnp.log2(dt_c), NEG_BIG)
        c0, c1, c2 = _split_bf16(cum_c, 3)
        l0, l1 = _split_bf16(ldt, 2)
        u = jnp.where(piece == 0, c0, jnp.where(piece == 1, c1, jnp.where(piece == 2, c2, jnp.where(piece < 8, 1.0, 0.0))))
        v = jnp.where(piece < 3, 1.0, jnp.where(piece == 3, -c0, jnp.where(piece == 4, -c1, jnp.where(
            piece == 5, -c2, jnp.where(piece == 6, l0, jnp.where(piece == 7, l1, 0.0))))))
        vst = jnp.concatenate([jnp.where(head == m, v, 0.0) for m in range(2 * SSD_HPG)], axis=0).astype(BF16)
        return lax.dot_general(u.astype(BF16), vst, NT_DIMS, preferred_element_type=F32)

    def intra_weights(s, arg_all):
        lower = ii >= jj
        ws = []
        for r in range(SSD_HPG):
            arg = jnp.where(lower, arg_all[:, r * q:(r + 1) * q], arg_all[:, (SSD_HPG + r) * q:(SSD_HPG + r + 1) * q])
            ws.append((s * jnp.exp2(arg)).astype(BF16))
        return jnp.concatenate(ws, axis=1)

    def finish(r0, y, yi):
        ecum = jnp.exp2(cum_ref[pl.ds(r0, q), 0:ex])
        y = y + yi[:, 0:gw] * ecum[:, 0:gw] + yi[:, gw:ex] * ecum[:, gw:ex]
        y = (y + dsk * xc_ref[pl.ds(r0, q), :]) * _silu(z_ref[pl.ds(r0, q), :].astype(F32))
        y = y * lax.rsqrt(jnp.mean(y * y, axis=-1, keepdims=True) + EPS) * nw
        o_ref[pl.ds(r0, q), :] = y.astype(o_ref.dtype)

    def pass_c(grp_i, _):
        cs_ = [SSD_GROUP_C * grp_i + n for n in range(SSD_GROUP_C)]
        starts = [pl.multiple_of(c * q, q) for c in cs_]
        args = [exponents(r0) for r0 in starts]
        cbs = [cs_ref[pl.ds(r0, q), :] for r0 in starts]
        scores = [lax.dot_general(cb, bs_ref[pl.ds(r0, q), :], NT_DIMS, preferred_element_type=F32)
                  for cb, r0 in zip(cbs, starts)]
        yis = [jnp.dot(cb, sall_ref[c], preferred_element_type=F32) for cb, c in zip(cbs, cs_)]
        wcat = [intra_weights(s, a) for s, a in zip(scores, args)]
        ys = [jnp.dot(w, xm_ref[pl.ds(pl.multiple_of(c * SSD_HPG * q, q), SSD_HPG * q), :], preferred_element_type=F32)
              for w, c in zip(wcat, cs_)]
        for r0, y, yi in zip(starts, ys, yis):
            finish(r0, y, yi)
        return 0

    lax.fori_loop(0, n_q // SSD_GROUP_C, pass_c, 0)


def _ssd_constants():
    rows, q, tab, ex = CONV_ROWS, SSD_CHUNK, SSD_TAB, SSD_EXP
    t = np.arange(rows)
    shm = np.concatenate([(t[None, :] == t[:, None] + o) for o in CONV_SHIFTS], axis=0)
    tq = np.arange(q)
    tri = np.tile(tq[None, :] <= tq[:, None], (1, 2))
    lane = np.arange(tab)
    want = np.where(lane < ex, 4 * (lane // SSD_GW) + (lane % SSD_GW) // SSD_HEADDIM,
                    np.where(lane < ex + 64, (lane - ex) % 8, -1000))
    src = np.arange(2 * DT_LANES) % DT_LANES
    emat = np.stack([(src[:, None] - 8 * g) == want[None, :] for g in range(SSD_GROUPS)], axis=0)
    as_bf16 = lambda a: jnp.asarray(a.astype(np.float32), dtype=BF16)
    return as_bf16(shm), as_bf16(tri), as_bf16(emat)


def _ssd(proj_l, proj_c, dt_l, dt_c, conv_w, conv_b, bias, alog, dsk, nw):
    b, seq, _ = proj_l.shape
    ctx_len = proj_c.shape[1]
    assert seq % CONV_ROWS == 0 and ctx_len % CONV_ROWS == 0 and CONV_ROWS % SSD_CHUNK == 0
    gw, ns, q, ex, tab = SSD_GW, SSD_STATE, SSD_CHUNK, SSD_EXP, SSD_TAB
    n_q, n_qc = seq // q, ctx_len // q
    cx, cb_, cc_ = 0, SSD_INNER // ns, (SSD_INNER + SSD_BC) // ns
    shm, tri, emat = _ssd_constants()
    return pl.pallas_call(
        _ssd_kernel,
        out_shape=jax.ShapeDtypeStruct((b, seq, SSD_INNER), BF16),
        grid=(b, SSD_GROUPS),
        in_specs=[pl.BlockSpec((None, seq, gw), lambda i, g: (i, 0, COL_XS // gw + g)),
                  pl.BlockSpec((None, seq, ns), lambda i, g: (i, 0, COL_B // ns + g)),
                  pl.BlockSpec((None, seq, ns), lambda i, g: (i, 0, COL_C // ns + g)),
                  pl.BlockSpec((None, seq, gw), lambda i, g: (i, 0, COL_Z // gw + g)),
                  pl.BlockSpec((None, seq, DT_LANES), lambda i, g: (i, 0, 0)),
                  pl.BlockSpec((None, ctx_len, gw), lambda i, g: (i, 0, CTX_XS // gw + g)),
                  pl.BlockSpec((None, ctx_len, ns), lambda i, g: (i, 0, CTX_B // ns + g)),
                  pl.BlockSpec((None, ctx_len, DT_LANES), lambda i, g: (i, 0, 0)),
                  pl.BlockSpec((SSD_CONV_W, gw), lambda i, g: (0, cx + g)),
                  pl.BlockSpec((SSD_CONV_W, ns), lambda i, g: (0, cb_ + g)),
                  pl.BlockSpec((SSD_CONV_W, ns), lambda i, g: (0, cc_ + g)),
                  pl.BlockSpec((1, gw), lambda i, g: (0, cx + g)),
                  pl.BlockSpec((1, ns), lambda i, g: (0, cb_ + g)),
                  pl.BlockSpec((1, ns), lambda i, g: (0, cc_ + g)),
                  pl.BlockSpec((1, DT_LANES), lambda i, g: (0, 0)),
                  pl.BlockSpec((None, 1, tab), lambda i, g: (g, 0, 0)),
                  pl.BlockSpec((1, gw), lambda i, g: (0, g)),
                  pl.BlockSpec((1, gw), lambda i, g: (0, g)),
                  pl.BlockSpec(shm.shape, lambda i, g: (0, 0)),
                  pl.BlockSpec(tri.shape, lambda i, g: (0, 0)),
                  pl.BlockSpec((None,) + emat.shape[1:], lambda i, g: (g, 0, 0))],
        out_specs=pl.BlockSpec((None, seq, gw), lambda i, g: (i, 0, g)),
        scratch_shapes=[pltpu.VMEM((seq + 2 * CONV_HALO, gw + 2 * ns), BF16),
                        pltpu.VMEM((seq, gw), F32),
                        pltpu.VMEM((n_q * SSD_HPG * q, gw), BF16),
                        pltpu.VMEM((seq, ns), BF16), pltpu.VMEM((seq, ns), BF16),
                        pltpu.VMEM((seq, tab), F32), pltpu.VMEM((seq, tab), F32),
                        pltpu.VMEM((n_q, ns, ex), F32), pltpu.VMEM((n_q * 8, ex), F32),
                        pltpu.VMEM((n_q, ns, ex), BF16),
                        pltpu.VMEM((ctx_len + 2 * CONV_HALO, gw + ns), BF16),
                        pltpu.VMEM((ctx_len, gw), F32), pltpu.VMEM((ctx_len, ns), BF16),
                        pltpu.VMEM((ctx_len, tab), F32),
                        pltpu.VMEM((n_qc, ns, ex), F32), pltpu.VMEM((n_qc * 8, ex), F32),
                        pltpu.VMEM((ns, ex), F32)],
        compiler_params=pltpu.CompilerParams(dimension_semantics=("parallel", "arbitrary"), vmem_limit_bytes=VMEM_LIMIT),
        name="ssd",
    )(proj_l, proj_l, proj_l, proj_l, dt_l, proj_c, proj_c, dt_c,
      conv_w, conv_w, conv_w, conv_b, conv_b, conv_b, bias, alog, dsk, nw, shm, tri, emat)


def _merge_kernel(yr_ref, ys_ref, gt_ref, x_ref, gate_ref, wr_ref, ws_ref, wo_ref, nw_ref, o_ref):
    o_r = jnp.dot(yr_ref[...], wr_ref[...], preferred_element_type=F32)
    o_s = jnp.dot(ys_ref[...], ws_ref[...], preferred_element_type=F32)
    gt = gt_ref[...].astype(F32)
    m = jax.nn.sigmoid(gt[:, :D_MODEL]) * o_r + jax.nn.sigmoid(gt[:, D_MODEL:]) * o_s
    out = jnp.dot(m.astype(BF16), wo_ref[...], preferred_element_type=F32)
    nrm = out * lax.rsqrt(jnp.mean(out * out, axis=-1, keepdims=True) + EPS) * nw_ref[...]
    o_ref[...] = x_ref[...] + gate_ref[...] * nrm


def _merge(yr, ys, proj_l, x2, gate, w_ret_o, w_ssd_o, w_out, nw, seq_len):
    m = x2.shape[0]
    tm = min(512, seq_len)
    per = seq_len // tm
    const = dict(pipeline_mode=pl.Buffered(1))
    return pl.pallas_call(
        _merge_kernel,
        out_shape=jax.ShapeDtypeStruct((m, D_MODEL), F32),
        grid=(m // tm,),
        in_specs=[pl.BlockSpec((tm, RET_V), lambda i: (i, 0)),
                  pl.BlockSpec((tm, SSD_INNER), lambda i: (i, 0)),
                  pl.BlockSpec((tm, 2 * D_MODEL), lambda i: (i, COL_GATES // (2 * D_MODEL))),
                  pl.BlockSpec((tm, D_MODEL), lambda i: (i, 0)),
                  pl.BlockSpec((None, 1, D_MODEL), lambda i: (i // per, 0, 0)),
                  pl.BlockSpec((RET_V, D_MODEL), lambda i: (0, 0), **const),
                  pl.BlockSpec((SSD_INNER, D_MODEL), lambda i: (0, 0), **const),
                  pl.BlockSpec((D_MODEL, D_MODEL), lambda i: (0, 0), **const),
                  pl.BlockSpec((1, D_MODEL), lambda i: (0, 0))],
        out_specs=pl.BlockSpec((tm, D_MODEL), lambda i: (i, 0)),
        compiler_params=pltpu.CompilerParams(dimension_semantics=("parallel",), vmem_limit_bytes=VMEM_LIMIT),
        name="merge",
    )(yr, ys, proj_l, x2, gate, w_ret_o, w_ssd_o, w_out, nw)


def _rope_tables(n_tokens):
    pos = jnp.arange(n_tokens)
    row = (pos // GRID_W).astype(F32)
    col = (pos % GRID_W).astype(F32)
    inv_freq = ROPE_THETA ** (-jnp.arange(ROPE_FREQS, dtype=F32) / ROPE_FREQS)
    ar = row[:, None] * inv_freq
    ac = col[:, None] * inv_freq
    cos_t = jnp.concatenate([jnp.cos(ar), jnp.cos(ar), jnp.cos(ac), jnp.cos(ac)], axis=1)
    sin_t = jnp.concatenate([-jnp.sin(ar), jnp.sin(ar), -jnp.sin(ac), jnp.sin(ac)], axis=1)
    return cos_t, sin_t


def _group_major(t):
    lead = t.shape[:-2]
    t = t.reshape(lead + (2, SSD_GROUPS, SSD_HPG))
    t = jnp.moveaxis(t, -3, -2)
    return t.reshape(lead + (SSD_GROUPS, 2 * SSD_HPG))


def _table_lanes(t):
    gm = _group_major(t)
    expanded = jnp.repeat(gm, SSD_HEADDIM, axis=1)
    compact = jnp.pad(jnp.tile(gm, (1, 8)), ((0, 0), (0, 64)))
    return jnp.concatenate([expanded, compact], axis=1)[:, None, :]


def kernel(x, c, ctx, c_ctx, w_mod, b_mod, norm_pre_w, norm_post_w, w_in, ret_decay, ret_gn_w, ssd_conv_w, ssd_conv_b,
           ssd_dt_bias, ssd_a_log, ssd_D, ssd_norm_w, w_ret_o, w_ssd_o, w_out):
    assert w_in.shape[0] == 1, "single-layer problem: the context stream is never updated"
    b, seq, d = x.shape
    ctx_len = ctx.shape[1]

    offs = [0]
    for s in IN_SIZES:
        offs.append(offs[-1] + s)
    dt_lo, dt_hi = offs[6], offs[7]
    w_main = jnp.concatenate([w_in[0][:, :dt_lo], w_in[0][:, dt_hi:]], axis=1).astype(BF16)
    wdt = _group_major(w_in[0][:, dt_lo:dt_hi].reshape(d, 2, SSD_HEADS)).reshape(d, 2 * SSD_HEADS)
    wdt = jnp.pad(wdt, ((0, 0), (0, DT_LANES - 2 * SSD_HEADS))).astype(BF16)
    bias = jnp.pad(_group_major(ssd_dt_bias[0]).reshape(1, 2 * SSD_HEADS), ((0, 0), (0, DT_LANES - 2 * SSD_HEADS)))
    alog = _table_lanes(ssd_a_log[0])
    dsk = jnp.repeat(ssd_D[0], SSD_HEADDIM)[None, :]
    dec = jnp.broadcast_to(ret_decay[0].T[:, :, None], (RET_HEADS, 2, 128))

    n_rows = -(-(b + 1) // 8) * 8
    c_all = jnp.concatenate([c, c_ctx[None, :], jnp.zeros((n_rows - b - 1, d), F32)], axis=0)
    mod = _modulation(c_all, w_mod[0], b_mod[0][None, :])
    shift, scale, gate = mod[:, :d], mod[:, d:2 * d], mod[:, 2 * d:]
    nw_pre = norm_pre_w[0][None, :]

    x2 = x.reshape(b * seq, d)
    proj_l, dt_l = _inproj(x2, shift[:b, None, :], scale[:b, None, :], nw_pre, w_main, wdt, seq, N_MAIN, "inproj_latent")
    proj_c, dt_c = _inproj(ctx.reshape(b * ctx_len, d), shift[b:b + 1, None, :], scale[b:b + 1, None, :], nw_pre,
                           w_main, wdt, ctx_len, N_CTX, "inproj_ctx")
    proj_l = proj_l.reshape(b, seq, N_MAIN)
    proj_c = proj_c.reshape(b, ctx_len, N_CTX)
    dt_l = dt_l.reshape(b, seq, DT_LANES)
    dt_c = dt_c.reshape(b, ctx_len, DT_LANES)

    cos_t, sin_t = _rope_tables(seq)
    yr = _retention(proj_l, proj_c, cos_t, sin_t, dec, ret_gn_w[0][None, :])
    ys = _ssd(proj_l, proj_c, dt_l, dt_c, ssd_conv_w[0], ssd_conv_b[0][None, :], bias, alog, dsk, ssd_norm_w[0][None, :])

    out = _merge(yr.reshape(b * seq, RET_V), ys.reshape(b * seq, SSD_INNER), proj_l.reshape(b * seq, N_MAIN), x2,
                 gate[:b, None, :], w_ret_o[0].astype(BF16), w_ssd_o[0].astype(BF16), w_out[0].astype(BF16),
                 norm_post_w[0][None, :], seq)
    return out.reshape(b, seq, d)
```

```python
import math

import jax
import jax.numpy as jnp
import numpy as np
from jax import lax
from jax.experimental import pallas as pl
from jax.experimental.pallas import tpu as pltpu

F32 = jnp.float32
BF16 = jnp.bfloat16
HIGHEST = lax.Precision.HIGHEST

D_MODEL = 1024
EPS = 1e-6
GRID_W = 64
ROPE_THETA = 10000.0
ROPE_FREQS = 64

RET_HEADS = 4
RET_DK = 256
RET_DV = 512
RET_QK = RET_HEADS * RET_DK
RET_V = RET_HEADS * RET_DV
RET_CHUNK = 256
RET_GROUP = 4

SSD_INNER = 2048
SSD_HEADDIM = 64
SSD_HEADS = 32
SSD_GROUPS = 8
SSD_HPG = 4
SSD_STATE = 128
SSD_BC = SSD_GROUPS * SSD_STATE
SSD_CONV_W = 5
SSD_GW = SSD_HPG * SSD_HEADDIM
SSD_CHUNK = 128
SSD_GROUP_A = 8
SSD_GROUP_C = 4
CONV_ROWS = 256
CONV_GROUP = 4
CONV_HALO = 16
CONV_SHIFTS = tuple(o for o in range(-(SSD_CONV_W // 2), SSD_CONV_W // 2 + 1) if o != 0)
DT_LANES = 128
SSD_EXP = 2 * SSD_GW
SSD_TAB = SSD_EXP + 128
NEG_BIG = -1e30
LOG2_E = 1.4426950408889634

IN_SIZES = (RET_QK, RET_QK, RET_V, RET_V, SSD_INNER, SSD_INNER + 2 * SSD_BC, 2 * SSD_HEADS, 2 * D_MODEL)

COL_Q = 0
COL_K = COL_Q + RET_QK
COL_V = COL_K + RET_QK
COL_G = COL_V + RET_V
COL_Z = COL_G + RET_V
COL_XS = COL_Z + SSD_INNER
COL_B = COL_XS + SSD_INNER
COL_C = COL_B + SSD_BC
COL_GATES = COL_C + SSD_BC
N_MAIN = COL_GATES + 2 * D_MODEL
CTX_K = 0
CTX_V = CTX_K + RET_QK
CTX_XS = CTX_V + RET_V
CTX_B = CTX_XS + SSD_INNER
CTX_C = CTX_B + SSD_BC
N_CTX = CTX_C + SSD_BC
INPROJ_TN = 1024
CTX_TILES_KV = (CTX_XS - CTX_K) // INPROJ_TN
CTX_OFF_KV = COL_K // INPROJ_TN
CTX_OFF_XBC = (COL_XS - CTX_XS) // INPROJ_TN

VMEM_LIMIT = 48 * 1024 * 1024

NT_DIMS = (((1,), (1,)), ((), ()))
TN_DIMS = (((0,), (0,)), ((), ()))


def _silu(x):
    return x * jax.nn.sigmoid(x)


def _softplus(x):
    return jnp.maximum(x, 0.0) + jnp.log1p(jnp.exp(-jnp.abs(x)))


def _log_sigmoid(x):
    return jnp.minimum(x, 0.0) - jnp.log1p(jnp.exp(-jnp.abs(x)))


def _split_bf16(a, parts):
    out = []
    for _ in range(parts):
        p = a.astype(BF16).astype(F32)
        out.append(p)
        a = a - p
    return out


def _mod_kernel(c_ref, w_ref, b_ref, o_ref):
    o_ref[...] = jnp.dot(_silu(c_ref[...]), w_ref[...], preferred_element_type=F32, precision=HIGHEST) + b_ref[...]


def _modulation(c_all, w_mod, b_mod):
    rows = c_all.shape[0]
    n = w_mod.shape[1]
    tn = D_MODEL
    return pl.pallas_call(
        _mod_kernel,
        out_shape=jax.ShapeDtypeStruct((rows, n), F32),
        grid=(n // tn,),
        in_specs=[pl.BlockSpec((rows, D_MODEL), lambda j: (0, 0)),
                  pl.BlockSpec((D_MODEL, tn), lambda j: (0, j)),
                  pl.BlockSpec((1, tn), lambda j: (0, j))],
        out_specs=pl.BlockSpec((rows, tn), lambda j: (0, j)),
        compiler_params=pltpu.CompilerParams(dimension_semantics=("arbitrary",), vmem_limit_bytes=VMEM_LIMIT),
        name="mod",
    )(c_all, w_mod, b_mod)


def _inproj_kernel(x_ref, shift_ref, scale_ref, nw_ref, w_ref, wdt_ref, o_ref, dt_ref, h_ref):
    @pl.when(pl.program_id(1) == 0)
    def _():
        x = x_ref[...]
        r = lax.rsqrt(jnp.mean(x * x, axis=-1, keepdims=True) + EPS)
        h = (x * r * nw_ref[...] * (1.0 + scale_ref[...]) + shift_ref[...]).astype(BF16)
        h_ref[...] = h
        dt_ref[...] = jnp.dot(h, wdt_ref[...], preferred_element_type=F32)

    o_ref[...] = jnp.dot(h_ref[...], w_ref[...], preferred_element_type=F32).astype(o_ref.dtype)


def _inproj(x2, shift, scale, nw, w, wdt, seq_len, n_cols, name):
    m = x2.shape[0]
    nb = shift.shape[0]
    tm = min(2048, seq_len if nb > 1 else m)
    tn = INPROJ_TN
    per = seq_len // tm
    mod_idx = (lambda i, j: (i // per, 0, 0)) if nb > 1 else (lambda i, j: (0, 0, 0))
    if n_cols == N_MAIN:
        w_idx = lambda i, j: (0, j)
    else:
        w_idx = lambda i, j: (0, jnp.where(j < CTX_TILES_KV, j + CTX_OFF_KV, j + CTX_OFF_XBC))
    return pl.pallas_call(
        _inproj_kernel,
        out_shape=(jax.ShapeDtypeStruct((m, n_cols), BF16), jax.ShapeDtypeStruct((m, DT_LANES), F32)),
        grid=(m // tm, n_cols // tn),
        in_specs=[pl.BlockSpec((tm, D_MODEL), lambda i, j: (i, 0)),
                  pl.BlockSpec((None, 1, D_MODEL), mod_idx),
                  pl.BlockSpec((None, 1, D_MODEL), mod_idx),
                  pl.BlockSpec((1, D_MODEL), lambda i, j: (0, 0)),
                  pl.BlockSpec((D_MODEL, tn), w_idx),
                  pl.BlockSpec((D_MODEL, DT_LANES), lambda i, j: (0, 0))],
        out_specs=(pl.BlockSpec((tm, tn), lambda i, j: (i, j)),
                   pl.BlockSpec((tm, DT_LANES), lambda i, j: (i, 0))),
        scratch_shapes=[pltpu.VMEM((tm, D_MODEL), BF16)],
        compiler_params=pltpu.CompilerParams(dimension_semantics=("parallel", "arbitrary"), vmem_limit_bytes=VMEM_LIMIT),
        name=name,
    )(x2, shift, scale, nw, w, wdt)


def _ret_kernel(q_ref, k_ref, v_ref, g_ref, kc_ref, vc_ref, cos_ref, sin_ref, dec_ref, gnw_ref, o_ref,
                qs_ref, qin_ref, ks_ref, kend_ref, rhs_ref, sf_ref, sb_ref, dm_ref, pd_ref):
    seq = q_ref.shape[0]
    ch = RET_CHUNK
    n_ch = seq // ch
    dk = RET_DK

    lg = _log_sigmoid(dec_ref[...])
    lgf = lg[0:1, 0:1]
    lgb = lg[1:2, 0:1]

    ii = lax.broadcasted_iota(jnp.int32, (ch, ch), 0)
    jj = lax.broadcasted_iota(jnp.int32, (ch, ch), 1)
    dm_ref[...] = jnp.exp(jnp.abs(ii - jj).astype(F32) * jnp.where(ii >= jj, lgf, lgb))

    pos = lax.broadcasted_iota(jnp.int32, (ch, 128), 0).astype(F32)
    pd_ref[0] = jnp.exp((pos + 1.0) * lgf)
    pd_ref[1] = jnp.exp((ch - pos) * lgb)
    pd_ref[2] = jnp.exp((ch - 1.0 - pos) * lgf)
    pd_ref[3] = jnp.exp(pos * lgb)
    all_f = jnp.exp(ch * lgf)
    all_b = jnp.exp(ch * lgb)

    def rope_rows(c, _):
        r0 = pl.multiple_of(c * ch, ch)
        for s in (0, 128):
            cos_h = cos_ref[pl.ds(r0, ch), s:s + 128]
            sin_h = sin_ref[pl.ds(r0, ch), s:s + 128]
            qh = q_ref[pl.ds(r0, ch), s:s + 128].astype(F32)
            qr = (qh * cos_h + pltpu.roll(qh, 64, 1) * sin_h) * (RET_DK ** -0.5)
            qs_ref[pl.ds(r0, ch), s:s + 128] = qr.astype(BF16)
            qin_ref[pl.ds(r0, ch), s:s + 128] = (qr * pd_ref[0]).astype(BF16)
            qin_ref[pl.ds(r0, ch), dk + s:dk + s + 128] = (qr * pd_ref[1]).astype(BF16)
            kh = k_ref[pl.ds(r0, ch), s:s + 128].astype(F32)
            kr = kh * cos_h + pltpu.roll(kh, 64, 1) * sin_h
            ks_ref[pl.ds(r0, ch), s:s + 128] = kr.astype(BF16)
            kend_ref[pl.ds(r0, ch), s:s + 128] = (kr * pd_ref[2]).astype(BF16)
            kend_ref[pl.ds(r0, ch), dk + s:dk + s + 128] = (kr * pd_ref[3]).astype(BF16)
        return 0

    lax.fori_loop(0, n_ch, rope_rows, 0)

    def outer(kb, vb):
        return lax.dot_general(kb, vb, TN_DIMS, preferred_element_type=F32)

    kc = kc_ref[...].astype(F32)
    end_f = jnp.concatenate([pd_ref[2], pd_ref[2]], axis=1)
    end_b = jnp.concatenate([pd_ref[3], pd_ref[3]], axis=1)
    sf_ref[...] = outer((kc * end_f).astype(BF16), vc_ref[...])
    sb_ref[...] = outer((kc * end_b).astype(BF16), vc_ref[...])

    def states(t, _):
        cf = t
        cb = n_ch - 1 - t
        rf = pl.multiple_of(cf * ch, ch)
        rb = pl.multiple_of(cb * ch, ch)
        vf = v_ref[pl.ds(rf, ch), :]
        rhs_ref[cf, 0:ch, :] = vf
        rhs_ref[cf, ch:ch + dk, :] = sf_ref[...].astype(BF16)
        rhs_ref[cb, ch + dk:ch + 2 * dk, :] = sb_ref[...].astype(BF16)
        sf_ref[...] = sf_ref[...] * all_f + outer(kend_ref[pl.ds(rf, ch), 0:dk], vf)
        sb_ref[...] = sb_ref[...] * all_b + outer(kend_ref[pl.ds(rb, ch), dk:2 * dk], v_ref[pl.ds(rb, ch), :])
        return 0

    lax.fori_loop(0, n_ch, states, 0, unroll=2)

    group = math.gcd(RET_GROUP, n_ch)

    def outputs(pair, _):
        cs_ = [group * pair + n for n in range(group)]
        starts = [pl.multiple_of(c * ch, ch) for c in cs_]
        scs = [lax.dot_general(qs_ref[pl.ds(r0, ch), :], ks_ref[pl.ds(r0, ch), :], NT_DIMS, preferred_element_type=F32)
               for r0 in starts]
        lhs = [jnp.concatenate([(sc * dm_ref[...]).astype(BF16), qin_ref[pl.ds(r0, ch), :]], axis=1)
               for sc, r0 in zip(scs, starts)]
        ys = [jnp.dot(l, rhs_ref[c], preferred_element_type=F32) for l, c in zip(lhs, cs_)]
        for r0, y in zip(starts, ys):
            mu = jnp.mean(y, axis=-1, keepdims=True)
            yc = y - mu
            var = jnp.mean(yc * yc, axis=-1, keepdims=True)
            g = g_ref[pl.ds(r0, ch), :].astype(F32)
            o_ref[pl.ds(r0, ch), :] = (yc * lax.rsqrt(var + EPS) * gnw_ref[...] * _silu(g)).astype(o_ref.dtype)
        return 0

    lax.fori_loop(0, n_ch // group, outputs, 0)


def _retention(proj_l, proj_c, cos_t, sin_t, dec, gnw):
    b, seq, _ = proj_l.shape
    ctx_len = proj_c.shape[1]
    assert ctx_len == RET_CHUNK and seq % (2 * RET_CHUNK) == 0
    n_ch = seq // RET_CHUNK
    kq, kv = RET_DK, RET_DV
    const = dict(pipeline_mode=pl.Buffered(1))
    return pl.pallas_call(
        _ret_kernel,
        out_shape=jax.ShapeDtypeStruct((b, seq, RET_V), BF16),
        grid=(b, RET_HEADS),
        in_specs=[pl.BlockSpec((None, seq, kq), lambda i, h: (i, 0, COL_Q // kq + h)),
                  pl.BlockSpec((None, seq, kq), lambda i, h: (i, 0, COL_K // kq + h)),
                  pl.BlockSpec((None, seq, kv), lambda i, h: (i, 0, COL_V // kv + h)),
                  pl.BlockSpec((None, seq, kv), lambda i, h: (i, 0, COL_G // kv + h)),
                  pl.BlockSpec((None, ctx_len, kq), lambda i, h: (i, 0, CTX_K // kq + h)),
                  pl.BlockSpec((None, ctx_len, kv), lambda i, h: (i, 0, CTX_V // kv + h)),
                  pl.BlockSpec((seq, kq), lambda i, h: (0, 0), **const),
                  pl.BlockSpec((seq, kq), lambda i, h: (0, 0), **const),
                  pl.BlockSpec((None, 2, 128), lambda i, h: (h, 0, 0)),
                  pl.BlockSpec((1, kv), lambda i, h: (0, h))],
        out_specs=pl.BlockSpec((None, seq, kv), lambda i, h: (i, 0, h)),
        scratch_shapes=[pltpu.VMEM((seq, kq), BF16), pltpu.VMEM((seq, 2 * kq), BF16),
                        pltpu.VMEM((seq, kq), BF16), pltpu.VMEM((seq, 2 * kq), BF16),
                        pltpu.VMEM((n_ch, RET_CHUNK + 2 * kq, kv), BF16),
                        pltpu.VMEM((kq, kv), F32), pltpu.VMEM((kq, kv), F32),
                        pltpu.VMEM((RET_CHUNK, RET_CHUNK), F32), pltpu.VMEM((4, RET_CHUNK, 128), F32)],
        compiler_params=pltpu.CompilerParams(dimension_semantics=("parallel", "arbitrary"), vmem_limit_bytes=VMEM_LIMIT),
        name="ret",
    )(proj_l, proj_l, proj_l, proj_l, proj_c, proj_c, cos_t, sin_t, dec, gnw)


def _ssd_kernel(xs_ref, b_ref, c_ref, z_ref, dt_ref, xsc_ref, bc_ref, dtc_ref,
                cwx_ref, cwb_ref, cwc_ref, cbx_ref, cbb_ref, cbc_ref, bias_ref, alog_ref, dsk_ref, nw_ref,
                shm_ref, tri_ref, emat_ref, hmask_ref, o_ref,
                pad_ref, xc_ref, xm_ref, bs_ref, cs_ref, dtv_ref, cum_ref, upd_ref, dec_ref, sall_ref,
                padc_ref, xcc_ref, bsc_ref, dtvc_ref, updc_ref, decc_ref, st_ref):
    seq = xs_ref.shape[0]
    ctx_len = xsc_ref.shape[0]
    q = SSD_CHUNK
    n_q = seq // q
    n_qc = ctx_len // q
    gw = SSD_GW
    ns = SSD_STATE
    ex = SSD_EXP
    tab = SSD_TAB
    halo = CONV_HALO

    lane_t = lax.broadcasted_iota(jnp.int32, (q, tab), 1)
    fwd_lane = (lane_t < gw) | ((lane_t >= ex) & (((lane_t - ex) & 7) < SSD_HPG))
    lane_c = lax.broadcasted_iota(jnp.int32, (q, 128), 1)
    piece = lane_c >> 3

    bias = bias_ref[...]
    neg_a = -jnp.exp(alog_ref[...]) * LOG2_E
    lane_h = lax.broadcasted_iota(jnp.int32, (q, 128), 1)

    def prep(pad, srcs, weights, biases, dts, nrows, dsts, dtv_dst, with_xm):
        width = pad.shape[1]
        pad[0:halo, :] = jnp.zeros((halo, width), BF16)
        pad[nrows + halo:nrows + 2 * halo, :] = jnp.zeros((halo, width), BF16)
        off = 0
        for s in srcs:
            w_ = s.shape[1]
            pad[halo:nrows + halo, off:off + w_] = s[...]
            off += w_
        cw = jnp.concatenate([w[...] for w in weights], axis=1)
        cb = jnp.concatenate([b_[...] for b_ in biases], axis=1)
        rows = CONV_ROWS
        half_w = SSD_CONV_W // 2
        strip_dst = []
        for dst in dsts:
            strip_dst += [(dst, o) for o in range(0, dst.shape[1], 128)]
        sub8 = lax.broadcasted_iota(jnp.int32, (8, 128), 0)

        def matmul_stage(i):
            r0 = pl.multiple_of(i * rows, rows)
            blk = pad[pl.ds(r0 + halo, rows), :]
            sh = jnp.dot(shm_ref[...], blk, preferred_element_type=F32)
            hi, lo = _split_bf16(_softplus(dts[pl.ds(r0, rows), :] + bias), 2)
            dtv_dst[pl.ds(r0, rows), :] = jnp.dot(jnp.concatenate([hi, lo], axis=1).astype(BF16), emat_ref[...],
                                                  preferred_element_type=F32)
            return r0, blk, sh

        def tap_stage(i, r0, blk, sh):
            prev = pad[pl.ds(r0, halo), :].astype(F32)[halo - 8:halo, :]
            nxt = pad[pl.ds(r0 + halo + rows, halo), :].astype(F32)[0:8, :]
            for s_, (dst, o) in enumerate(strip_dst):
                lo_, hi_ = s_ * 128, (s_ + 1) * 128
                acc = cb[:, lo_:hi_] + cw[half_w:half_w + 1, lo_:hi_] * blk[:, lo_:hi_].astype(F32)
                for n, off_ in enumerate(CONV_SHIFTS):
                    shf = sh[n * rows:(n + 1) * rows, lo_:hi_]
                    if off_ < 0:
                        edge = shf[0:8, :]
                        for e in range(-off_):
                            edge = jnp.where(sub8 == e, prev[8 + e + off_:9 + e + off_, lo_:hi_], edge)
                        shf = jnp.concatenate([edge, shf[8:, :]], axis=0)
                    else:
                        edge = shf[rows - 8:rows, :]
                        for e in range(off_):
                            edge = jnp.where(sub8 == 7 - e, nxt[off_ - 1 - e:off_ - e, lo_:hi_], edge)
                        shf = jnp.concatenate([shf[:rows - 8, :], edge], axis=0)
                    acc = acc + cw[off_ + half_w:off_ + half_w + 1, lo_:hi_] * shf
                y = _silu(acc)
                dst[pl.ds(r0, rows), o:o + 128] = y.astype(dst.dtype)
                if with_xm and dst is xc_ref:
                    for half in range(rows // q):
                        yh = y[half * q:(half + 1) * q, :]
                        for r in range(SSD_HPG):
                            row = pl.multiple_of(i * (rows // q) * SSD_HPG * q, q) + (half * SSD_HPG + r) * q
                            if r // 2 == s_:
                                keep = (lane_h < SSD_HEADDIM) if r % 2 == 0 else (lane_h >= SSD_HEADDIM)
                                xm_ref[pl.ds(row, q), o:o + 128] = jnp.where(keep, yh, 0.0).astype(BF16)
                            else:
                                xm_ref[pl.ds(row, q), o:o + 128] = jnp.zeros((q, 128), BF16)

        group = math.gcd(CONV_GROUP, nrows // rows)

        def conv(t, _):
            ids = [group * t + n for n in range(group)]
            staged = [matmul_stage(i) for i in ids]
            for i, st in zip(ids, staged):
                tap_stage(i, *st)
            return 0

        lax.fori_loop(0, nrows // rows // group, conv, 0)

    prep(pad_ref, (xs_ref, b_ref, c_ref), (cwx_ref, cwb_ref, cwc_ref), (cbx_ref, cbb_ref, cbc_ref),
         dt_ref, seq, (xc_ref, bs_ref, cs_ref), dtv_ref, True)
    prep(padc_ref, (xsc_ref, bc_ref), (cwx_ref, cwb_ref), (cbx_ref, cbb_ref),
         dtc_ref, ctx_len, (xcc_ref, bsc_ref), dtvc_ref, False)

    def tables(dtvs):
        a_s = [dtv * neg_a for dtv in dtvs]
        his, los = zip(*[_split_bf16(a, 2) for a in a_s])
        rhs = jnp.concatenate([jnp.concatenate(his, axis=1), jnp.concatenate(los, axis=1)], axis=0).astype(BF16)
        p_all = jnp.dot(tri_ref[...], rhs, preferred_element_type=F32)
        out = []
        for n, a in enumerate(a_s):
            p = p_all[:, n * tab:(n + 1) * tab]
            tot = p[q - 1:q, :]
            out.append((jnp.where(fwd_lane, p, tot - p + a), tot))
        return out

    def weighted_x(xcf, dtv, cum, tot):
        coef = dtv[:, 0:ex] * jnp.exp2(tot[:, 0:ex] - cum[:, 0:ex])
        return (jnp.concatenate([xcf, xcf], axis=1) * coef).astype(BF16)

    def contribution(bb, xw):
        return lax.dot_general(bb, xw, TN_DIMS, preferred_element_type=F32)

    def pass_a_pair(starts, load, cum_dst, upd_dst, dec_dst, chunk_ids):
        xcs, bbs, dtvs = zip(*[load(r0) for r0 in starts])
        tabs = tables(dtvs)
        xws = [weighted_x(xcf, dtv, cum, tot) for xcf, dtv, (cum, tot) in zip(xcs, dtvs, tabs)]
        for n, (r0, c) in enumerate(zip(starts, chunk_ids)):
            cum, tot = tabs[n]
            if cum_dst is not None:
                cum_dst[pl.ds(r0, q), :] = cum
            upd_dst[c] = contribution(bbs[n], xws[n])
            d0 = c * 8 if isinstance(c, int) else pl.multiple_of(c * 8, 8)
            dec_dst[pl.ds(d0, 8), :] = jnp.broadcast_to(jnp.exp2(tot[:, 0:ex]), (8, ex))

    assert n_qc == 2 and n_q % SSD_GROUP_A == 0 and n_q % SSD_GROUP_C == 0
    pass_a_pair([0, q], lambda r0: (xcc_ref[r0:r0 + q, :], bsc_ref[r0:r0 + q, :], dtvc_ref[r0:r0 + q, :]),
                None, updc_ref, decc_ref, [0, 1])

    def pass_a(grp_i, _):
        ids = [SSD_GROUP_A * grp_i + n for n in range(SSD_GROUP_A)]
        starts = [pl.multiple_of(c * q, q) for c in ids]
        pass_a_pair(starts, lambda r0: (xc_ref[pl.ds(r0, q), :], bs_ref[pl.ds(r0, q), :], dtv_ref[pl.ds(r0, q), :]),
                    cum_ref, upd_ref, dec_ref, ids)
        return 0

    lax.fori_loop(0, n_q // SSD_GROUP_A, pass_a, 0)

    st_ref[...] = jnp.zeros((ns, ex), F32)
    for cc in range(n_qc):
        st_ref[:, 0:gw] = st_ref[:, 0:gw] * decc_ref[cc * 8:cc * 8 + 1, 0:gw] + updc_ref[cc, :, 0:gw]
    for cc in reversed(range(n_qc)):
        st_ref[:, gw:ex] = st_ref[:, gw:ex] * decc_ref[cc * 8:cc * 8 + 1, gw:ex] + updc_ref[cc, :, gw:ex]

    def pass_b(t, _):
        cf = t
        cb = n_q - 1 - t
        sall_ref[cf, :, 0:gw] = st_ref[:, 0:gw].astype(BF16)
        sall_ref[cb, :, gw:ex] = st_ref[:, gw:ex].astype(BF16)
        df = dec_ref[pl.ds(pl.multiple_of(cf * 8, 8), 8), :]
        db = dec_ref[pl.ds(pl.multiple_of(cb * 8, 8), 8), :]
        st_ref[:, 0:gw] = st_ref[:, 0:gw] * df[0:1, 0:gw] + upd_ref[cf, :, 0:gw]
        st_ref[:, gw:ex] = st_ref[:, gw:ex] * db[0:1, gw:ex] + upd_ref[cb, :, gw:ex]
        return 0

    lax.fori_loop(0, n_q, pass_b, 0)

    ii = lax.broadcasted_iota(jnp.int32, (q, q), 0)
    jj = lax.broadcasted_iota(jnp.int32, (q, q), 1)
    dsk = dsk_ref[...]
    nw = nw_ref[...]

    def exponents(r0):
        cum_c = cum_ref[pl.ds(r0, q), ex:tab]
        dt_c = dtv_ref[pl.ds(r0, q), ex:tab]
        ldt = jnp.where(dt_c > 0.0, jnp.log2(dt_c), NEG_BIG)
        c0, c1 = _split_bf16(cum_c, 2)
        l0, l1 = _split_bf16(ldt, 2)
        u = jnp.where(piece == 0, c0, jnp.where(piece == 1, c1, jnp.where(piece < 6, 1.0, 0.0)))
        v = jnp.where(piece < 2, 1.0, jnp.where(piece == 2, -c0, jnp.where(piece == 3, -c1, jnp.where(
            piece == 4, l0, jnp.where(piece == 5, l1, 0.0)))))
        vb = v.astype(BF16)
        vst = jnp.concatenate([vb * hmask_ref[m] for m in range(2 * SSD_HPG)], axis=0)
        return lax.dot_general(u.astype(BF16), vst, NT_DIMS, preferred_element_type=F32)

    def intra_weights(s, arg_all):
        lower = ii >= jj
        ws = []
        for r in range(SSD_HPG):
            arg = jnp.where(lower, arg_all[:, r * q:(r + 1) * q], arg_all[:, (SSD_HPG + r) * q:(SSD_HPG + r + 1) * q])
            ws.append((s * jnp.exp2(arg)).astype(BF16))
        return jnp.concatenate(ws, axis=1)

    def finish(r0, y, yi):
        ecum = jnp.exp2(cum_ref[pl.ds(r0, q), 0:ex])
        y = y + yi[:, 0:gw] * ecum[:, 0:gw] + yi[:, gw:ex] * ecum[:, gw:ex]
        y = (y + dsk * xc_ref[pl.ds(r0, q), :]) * _silu(z_ref[pl.ds(r0, q), :].astype(F32))
        y = y * lax.rsqrt(jnp.mean(y * y, axis=-1, keepdims=True) + EPS) * nw
        o_ref[pl.ds(r0, q), :] = y.astype(o_ref.dtype)

    def pass_c(grp_i, _):
        cs_ = [SSD_GROUP_C * grp_i + n for n in range(SSD_GROUP_C)]
        starts = [pl.multiple_of(c * q, q) for c in cs_]
        args = [exponents(r0) for r0 in starts]
        cbs = [cs_ref[pl.ds(r0, q), :] for r0 in starts]
        scores = [lax.dot_general(cb, bs_ref[pl.ds(r0, q), :], NT_DIMS, preferred_element_type=F32)
                  for cb, r0 in zip(cbs, starts)]
        yis = [jnp.dot(cb, sall_ref[c], preferred_element_type=F32) for cb, c in zip(cbs, cs_)]
        wcat = [intra_weights(s, a) for s, a in zip(scores, args)]
        ys = [jnp.dot(w, xm_ref[pl.ds(pl.multiple_of(c * SSD_HPG * q, q), SSD_HPG * q), :], preferred_element_type=F32)
              for w, c in zip(wcat, cs_)]
        for r0, y, yi in zip(starts, ys, yis):
            finish(r0, y, yi)
        return 0

    lax.fori_loop(0, n_q // SSD_GROUP_C, pass_c, 0)


def _ssd_constants():
    rows, q, tab, ex = CONV_ROWS, SSD_CHUNK, SSD_TAB, SSD_EXP
    t = np.arange(rows)
    shm = np.concatenate([(t[None, :] == t[:, None] + o) for o in CONV_SHIFTS], axis=0)
    tq = np.arange(q)
    tri = np.tile(tq[None, :] <= tq[:, None], (1, 2))
    lane = np.arange(tab)
    want = np.where(lane < ex, 4 * (lane // SSD_GW) + (lane % SSD_GW) // SSD_HEADDIM,
                    np.where(lane < ex + 64, (lane - ex) % 8, -1000))
    src = np.arange(2 * DT_LANES) % DT_LANES
    emat = np.stack([(src[:, None] - 8 * g) == want[None, :] for g in range(SSD_GROUPS)], axis=0)
    as_bf16 = lambda a: jnp.asarray(a.astype(np.float32), dtype=BF16)
    hmask = np.broadcast_to((np.arange(128) % 8)[None, None, :] == np.arange(2 * SSD_HPG)[:, None, None],
                            (2 * SSD_HPG, q, 128))
    return as_bf16(shm), as_bf16(tri), as_bf16(emat), as_bf16(hmask)


def _ssd(proj_l, proj_c, dt_l, dt_c, conv_w, conv_b, bias, alog, dsk, nw):
    b, seq, _ = proj_l.shape
    ctx_len = proj_c.shape[1]
    assert seq % CONV_ROWS == 0 and ctx_len % CONV_ROWS == 0 and CONV_ROWS % SSD_CHUNK == 0
    gw, ns, q, ex, tab = SSD_GW, SSD_STATE, SSD_CHUNK, SSD_EXP, SSD_TAB
    n_q, n_qc = seq // q, ctx_len // q
    cx, cb_, cc_ = 0, SSD_INNER // ns, (SSD_INNER + SSD_BC) // ns
    shm, tri, emat, hmask = _ssd_constants()
    return pl.pallas_call(
        _ssd_kernel,
        out_shape=jax.ShapeDtypeStruct((b, seq, SSD_INNER), BF16),
        grid=(b, SSD_GROUPS),
        in_specs=[pl.BlockSpec((None, seq, gw), lambda i, g: (i, 0, COL_XS // gw + g)),
                  pl.BlockSpec((None, seq, ns), lambda i, g: (i, 0, COL_B // ns + g)),
                  pl.BlockSpec((None, seq, ns), lambda i, g: (i, 0, COL_C // ns + g)),
                  pl.BlockSpec((None, seq, gw), lambda i, g: (i, 0, COL_Z // gw + g)),
                  pl.BlockSpec((None, seq, DT_LANES), lambda i, g: (i, 0, 0)),
                  pl.BlockSpec((None, ctx_len, gw), lambda i, g: (i, 0, CTX_XS // gw + g)),
                  pl.BlockSpec((None, ctx_len, ns), lambda i, g: (i, 0, CTX_B // ns + g)),
                  pl.BlockSpec((None, ctx_len, DT_LANES), lambda i, g: (i, 0, 0)),
                  pl.BlockSpec((SSD_CONV_W, gw), lambda i, g: (0, cx + g)),
                  pl.BlockSpec((SSD_CONV_W, ns), lambda i, g: (0, cb_ + g)),
                  pl.BlockSpec((SSD_CONV_W, ns), lambda i, g: (0, cc_ + g)),
                  pl.BlockSpec((1, gw), lambda i, g: (0, cx + g)),
                  pl.BlockSpec((1, ns), lambda i, g: (0, cb_ + g)),
                  pl.BlockSpec((1, ns), lambda i, g: (0, cc_ + g)),
                  pl.BlockSpec((1, DT_LANES), lambda i, g: (0, 0)),
                  pl.BlockSpec((None, 1, tab), lambda i, g: (g, 0, 0)),
                  pl.BlockSpec((1, gw), lambda i, g: (0, g)),
                  pl.BlockSpec((1, gw), lambda i, g: (0, g)),
                  pl.BlockSpec(shm.shape, lambda i, g: (0, 0)),
                  pl.BlockSpec(tri.shape, lambda i, g: (0, 0)),
                  pl.BlockSpec((None,) + emat.shape[1:], lambda i, g: (g, 0, 0)),
                  pl.BlockSpec(hmask.shape, lambda i, g: (0, 0, 0))],
        out_specs=pl.BlockSpec((None, seq, gw), lambda i, g: (i, 0, g)),
        scratch_shapes=[pltpu.VMEM((seq + 2 * CONV_HALO, gw + 2 * ns), BF16),
                        pltpu.VMEM((seq, gw), F32),
                        pltpu.VMEM((n_q * SSD_HPG * q, gw), BF16),
                        pltpu.VMEM((seq, ns), BF16), pltpu.VMEM((seq, ns), BF16),
                        pltpu.VMEM((seq, tab), F32), pltpu.VMEM((seq, tab), F32),
                        pltpu.VMEM((n_q, ns, ex), F32), pltpu.VMEM((n_q * 8, ex), F32),
                        pltpu.VMEM((n_q, ns, ex), BF16),
                        pltpu.VMEM((ctx_len + 2 * CONV_HALO, gw + ns), BF16),
                        pltpu.VMEM((ctx_len, gw), F32), pltpu.VMEM((ctx_len, ns), BF16),
                        pltpu.VMEM((ctx_len, tab), F32),
                        pltpu.VMEM((n_qc, ns, ex), F32), pltpu.VMEM((n_qc * 8, ex), F32),
                        pltpu.VMEM((ns, ex), F32)],
        compiler_params=pltpu.CompilerParams(dimension_semantics=("parallel", "arbitrary"), vmem_limit_bytes=VMEM_LIMIT),
        name="ssd",
    )(proj_l, proj_l, proj_l, proj_l, dt_l, proj_c, proj_c, dt_c,
      conv_w, conv_w, conv_w, conv_b, conv_b, conv_b, bias, alog, dsk, nw, shm, tri, emat, hmask)


def _merge_kernel(yr_ref, ys_ref, gt_ref, x_ref, gate_ref, wr_ref, ws_ref, wo_ref, nw_ref, o_ref):
    o_r = jnp.dot(yr_ref[...], wr_ref[...], preferred_element_type=F32)
    o_s = jnp.dot(ys_ref[...], ws_ref[...], preferred_element_type=F32)
    gt = gt_ref[...].astype(F32)
    m = jax.nn.sigmoid(gt[:, :D_MODEL]) * o_r + jax.nn.sigmoid(gt[:, D_MODEL:]) * o_s
    out = jnp.dot(m.astype(BF16), wo_ref[...], preferred_element_type=F32)
    nrm = out * lax.rsqrt(jnp.mean(out * out, axis=-1, keepdims=True) + EPS) * nw_ref[...]
    o_ref[...] = x_ref[...] + gate_ref[...] * nrm


def _merge(yr, ys, proj_l, x2, gate, w_ret_o, w_ssd_o, w_out, nw, seq_len):
    m = x2.shape[0]
    tm = min(512, seq_len)
    per = seq_len // tm
    const = dict(pipeline_mode=pl.Buffered(1))
    return pl.pallas_call(
        _merge_kernel,
        out_shape=jax.ShapeDtypeStruct((m, D_MODEL), F32),
        grid=(m // tm,),
        in_specs=[pl.BlockSpec((tm, RET_V), lambda i: (i, 0)),
                  pl.BlockSpec((tm, SSD_INNER), lambda i: (i, 0)),
                  pl.BlockSpec((tm, 2 * D_MODEL), lambda i: (i, COL_GATES // (2 * D_MODEL))),
                  pl.BlockSpec((tm, D_MODEL), lambda i: (i, 0)),
                  pl.BlockSpec((None, 1, D_MODEL), lambda i: (i // per, 0, 0)),
                  pl.BlockSpec((RET_V, D_MODEL), lambda i: (0, 0), **const),
                  pl.BlockSpec((SSD_INNER, D_MODEL), lambda i: (0, 0), **const),
                  pl.BlockSpec((D_MODEL, D_MODEL), lambda i: (0, 0), **const),
                  pl.BlockSpec((1, D_MODEL), lambda i: (0, 0))],
        out_specs=pl.BlockSpec((tm, D_MODEL), lambda i: (i, 0)),
        compiler_params=pltpu.CompilerParams(dimension_semantics=("parallel",), vmem_limit_bytes=VMEM_LIMIT),
        name="merge",
    )(yr, ys, proj_l, x2, gate, w_ret_o, w_ssd_o, w_out, nw)


def _rope_tables(n_tokens):
    pos = jnp.arange(n_tokens)
    row = (pos // GRID_W).astype(F32)
    col = (pos % GRID_W).astype(F32)
    inv_freq = ROPE_THETA ** (-jnp.arange(ROPE_FREQS, dtype=F32) / ROPE_FREQS)
    ar = row[:, None] * inv_freq
    ac = col[:, None] * inv_freq
    cos_t = jnp.concatenate([jnp.cos(ar), jnp.cos(ar), jnp.cos(ac), jnp.cos(ac)], axis=1)
    sin_t = jnp.concatenate([-jnp.sin(ar), jnp.sin(ar), -jnp.sin(ac), jnp.sin(ac)], axis=1)
    return cos_t, sin_t


def _group_major(t):
    lead = t.shape[:-2]
    t = t.reshape(lead + (2, SSD_GROUPS, SSD_HPG))
    t = jnp.moveaxis(t, -3, -2)
    return t.reshape(lead + (SSD_GROUPS, 2 * SSD_HPG))


def _table_lanes(t):
    gm = _group_major(t)
    expanded = jnp.repeat(gm, SSD_HEADDIM, axis=1)
    compact = jnp.pad(jnp.tile(gm, (1, 8)), ((0, 0), (0, 64)))
    return jnp.concatenate([expanded, compact], axis=1)[:, None, :]


def kernel(x, c, ctx, c_ctx, w_mod, b_mod, norm_pre_w, norm_post_w, w_in, ret_decay, ret_gn_w, ssd_conv_w, ssd_conv_b,
           ssd_dt_bias, ssd_a_log, ssd_D, ssd_norm_w, w_ret_o, w_ssd_o, w_out):
    assert w_in.shape[0] == 1, "single-layer problem: the context stream is never updated"
    b, seq, d = x.shape
    ctx_len = ctx.shape[1]

    offs = [0]
    for s in IN_SIZES:
        offs.append(offs[-1] + s)
    dt_lo, dt_hi = offs[6], offs[7]
    w_main = jnp.concatenate([w_in[0][:, :dt_lo], w_in[0][:, dt_hi:]], axis=1).astype(BF16)
    wdt = _group_major(w_in[0][:, dt_lo:dt_hi].reshape(d, 2, SSD_HEADS)).reshape(d, 2 * SSD_HEADS)
    wdt = jnp.pad(wdt, ((0, 0), (0, DT_LANES - 2 * SSD_HEADS))).astype(BF16)
    bias = jnp.pad(_group_major(ssd_dt_bias[0]).reshape(1, 2 * SSD_HEADS), ((0, 0), (0, DT_LANES - 2 * SSD_HEADS)))
    alog = _table_lanes(ssd_a_log[0])
    dsk = jnp.repeat(ssd_D[0], SSD_HEADDIM)[None, :]
    dec = jnp.broadcast_to(ret_decay[0].T[:, :, None], (RET_HEADS, 2, 128))

    n_rows = -(-(b + 1) // 8) * 8
    c_all = jnp.concatenate([c, c_ctx[None, :], jnp.zeros((n_rows - b - 1, d), F32)], axis=0)
    mod = _modulation(c_all, w_mod[0], b_mod[0][None, :])
    shift, scale, gate = mod[:, :d], mod[:, d:2 * d], mod[:, 2 * d:]
    nw_pre = norm_pre_w[0][None, :]

    x2 = x.reshape(b * seq, d)
    proj_l, dt_l = _inproj(x2, shift[:b, None, :], scale[:b, None, :], nw_pre, w_main, wdt, seq, N_MAIN, "inproj_latent")
    proj_c, dt_c = _inproj(ctx.reshape(b * ctx_len, d), shift[b:b + 1, None, :], scale[b:b + 1, None, :], nw_pre,
                           w_main, wdt, ctx_len, N_CTX, "inproj_ctx")
    proj_l = proj_l.reshape(b, seq, N_MAIN)
    proj_c = proj_c.reshape(b, ctx_len, N_CTX)
    dt_l = dt_l.reshape(b, seq, DT_LANES)
    dt_c = dt_c.reshape(b, ctx_len, DT_LANES)

    cos_t, sin_t = _rope_tables(seq)
    yr = _retention(proj_l, proj_c, cos_t, sin_t, dec, ret_gn_w[0][None, :])
    ys = _ssd(proj_l, proj_c, dt_l, dt_c, ssd_conv_w[0], ssd_conv_b[0][None, :], bias, alog, dsk, ssd_norm_w[0][None, :])

    out = _merge(yr.reshape(b * seq, RET_V), ys.reshape(b * seq, SSD_INNER), proj_l.reshape(b * seq, N_MAIN), x2,
                 gate[:b, None, :], w_ret_o[0].astype(BF16), w_ssd_o[0].astype(BF16), w_out[0].astype(BF16),
                 norm_post_w[0][None, :], seq)
    return out.reshape(b, seq, d)
```

```python
import math

import jax
import jax.numpy as jnp
import numpy as np
from jax import lax
from jax.experimental import pallas as pl
from jax.experimental.pallas import tpu as pltpu

F32 = jnp.float32
BF16 = jnp.bfloat16
HIGHEST = lax.Precision.HIGHEST

D_MODEL = 1024
EPS = 1e-6
GRID_W = 64
ROPE_THETA = 10000.0
ROPE_FREQS = 64

RET_HEADS = 4
RET_DK = 256
RET_DV = 512
RET_QK = RET_HEADS * RET_DK
RET_V = RET_HEADS * RET_DV
RET_CHUNK = 256
RET_GROUP = 8
RET_ROPE_UNROLL = 2
RET_STATE_UNROLL = 8

SSD_INNER = 2048
SSD_HEADDIM = 64
SSD_HEADS = 32
SSD_GROUPS = 8
SSD_HPG = 4
SSD_STATE = 128
SSD_BC = SSD_GROUPS * SSD_STATE
SSD_CONV_W = 5
SSD_GW = SSD_HPG * SSD_HEADDIM
SSD_CHUNK = 128
SSD_GROUP_A = 8
SSD_GROUP_C = 8
CONV_ROWS = 256
CONV_GROUP = 4
CONV_HALO = 16
CONV_SHIFTS = tuple(o for o in range(-(SSD_CONV_W // 2), SSD_CONV_W // 2 + 1) if o != 0)
DT_LANES = 128
SSD_EXP = 2 * SSD_GW
SSD_TAB = SSD_EXP + 128
NEG_BIG = -1e30
LOG2_E = 1.4426950408889634

IN_SIZES = (RET_QK, RET_QK, RET_V, RET_V, SSD_INNER, SSD_INNER + 2 * SSD_BC, 2 * SSD_HEADS, 2 * D_MODEL)

COL_Q = 0
COL_K = COL_Q + RET_QK
COL_V = COL_K + RET_QK
COL_G = COL_V + RET_V
COL_Z = COL_G + RET_V
COL_XS = COL_Z + SSD_INNER
COL_B = COL_XS + SSD_INNER
COL_C = COL_B + SSD_BC
COL_GATES = COL_C + SSD_BC
N_MAIN = COL_GATES + 2 * D_MODEL
CTX_K = 0
CTX_V = CTX_K + RET_QK
CTX_XS = CTX_V + RET_V
CTX_B = CTX_XS + SSD_INNER
CTX_C = CTX_B + SSD_BC
N_CTX = CTX_C + SSD_BC
INPROJ_TN = 1024
CTX_TILES_KV = (CTX_XS - CTX_K) // INPROJ_TN
CTX_OFF_KV = COL_K // INPROJ_TN
CTX_OFF_XBC = (COL_XS - CTX_XS) // INPROJ_TN

VMEM_LIMIT = 48 * 1024 * 1024

NT_DIMS = (((1,), (1,)), ((), ()))
TN_DIMS = (((0,), (0,)), ((), ()))


def _silu(x):
    return x * jax.nn.sigmoid(x)


def _softplus(x):
    return jnp.maximum(x, 0.0) + jnp.log1p(jnp.exp(-jnp.abs(x)))


def _log_sigmoid(x):
    return jnp.minimum(x, 0.0) - jnp.log1p(jnp.exp(-jnp.abs(x)))


def _split_bf16(a, parts):
    out = []
    for _ in range(parts):
        p = a.astype(BF16).astype(F32)
        out.append(p)
        a = a - p
    return out


def _mod_kernel(c_ref, w_ref, b_ref, o_ref):
    o_ref[...] = jnp.dot(_silu(c_ref[...]), w_ref[...], preferred_element_type=F32, precision=HIGHEST) + b_ref[...]


def _modulation(c_all, w_mod, b_mod):
    rows = c_all.shape[0]
    n = w_mod.shape[1]
    tn = D_MODEL
    return pl.pallas_call(
        _mod_kernel,
        out_shape=jax.ShapeDtypeStruct((rows, n), F32),
        grid=(n // tn,),
        in_specs=[pl.BlockSpec((rows, D_MODEL), lambda j: (0, 0)),
                  pl.BlockSpec((D_MODEL, tn), lambda j: (0, j)),
                  pl.BlockSpec((1, tn), lambda j: (0, j))],
        out_specs=pl.BlockSpec((rows, tn), lambda j: (0, j)),
        compiler_params=pltpu.CompilerParams(dimension_semantics=("arbitrary",), vmem_limit_bytes=VMEM_LIMIT),
        name="mod",
    )(c_all, w_mod, b_mod)


def _inproj_kernel(x_ref, shift_ref, scale_ref, nw_ref, w_ref, wdt_ref, o_ref, dt_ref, h_ref):
    @pl.when(pl.program_id(1) == 0)
    def _():
        x = x_ref[...]
        r = lax.rsqrt(jnp.mean(x * x, axis=-1, keepdims=True) + EPS)
        h = (x * r * nw_ref[...] * (1.0 + scale_ref[...]) + shift_ref[...]).astype(BF16)
        h_ref[...] = h
        dt_ref[...] = jnp.dot(h, wdt_ref[...], preferred_element_type=F32)

    o_ref[...] = jnp.dot(h_ref[...], w_ref[...], preferred_element_type=F32).astype(o_ref.dtype)


def _inproj(x2, shift, scale, nw, w, wdt, seq_len, n_cols, name):
    m = x2.shape[0]
    nb = shift.shape[0]
    tm = min(2048, seq_len if nb > 1 else m)
    tn = INPROJ_TN
    per = seq_len // tm
    mod_idx = (lambda i, j: (i // per, 0, 0)) if nb > 1 else (lambda i, j: (0, 0, 0))
    if n_cols == N_MAIN:
        w_idx = lambda i, j: (0, j)
    else:
        w_idx = lambda i, j: (0, jnp.where(j < CTX_TILES_KV, j + CTX_OFF_KV, j + CTX_OFF_XBC))
    return pl.pallas_call(
        _inproj_kernel,
        out_shape=(jax.ShapeDtypeStruct((m, n_cols), BF16), jax.ShapeDtypeStruct((m, DT_LANES), F32)),
        grid=(m // tm, n_cols // tn),
        in_specs=[pl.BlockSpec((tm, D_MODEL), lambda i, j: (i, 0)),
                  pl.BlockSpec((None, 1, D_MODEL), mod_idx),
                  pl.BlockSpec((None, 1, D_MODEL), mod_idx),
                  pl.BlockSpec((1, D_MODEL), lambda i, j: (0, 0)),
                  pl.BlockSpec((D_MODEL, tn), w_idx),
                  pl.BlockSpec((D_MODEL, DT_LANES), lambda i, j: (0, 0))],
        out_specs=(pl.BlockSpec((tm, tn), lambda i, j: (i, j)),
                   pl.BlockSpec((tm, DT_LANES), lambda i, j: (i, 0))),
        scratch_shapes=[pltpu.VMEM((tm, D_MODEL), BF16)],
        compiler_params=pltpu.CompilerParams(dimension_semantics=("parallel", "arbitrary"), vmem_limit_bytes=VMEM_LIMIT),
        name=name,
    )(x2, shift, scale, nw, w, wdt)


def _ret_kernel(q_ref, k_ref, v_ref, g_ref, kc_ref, vc_ref, cos_ref, sin_ref, dec_ref, gnw_ref, o_ref,
                qs_ref, qin_ref, ks_ref, kend_ref, rhs_ref, sf_ref, sb_ref, dm_ref, pd_ref):
    seq = q_ref.shape[0]
    ch = RET_CHUNK
    n_ch = seq // ch
    dk = RET_DK

    lg = _log_sigmoid(dec_ref[...])
    lgf = lg[0:1, 0:1]
    lgb = lg[1:2, 0:1]

    ii = lax.broadcasted_iota(jnp.int32, (ch, ch), 0)
    jj = lax.broadcasted_iota(jnp.int32, (ch, ch), 1)
    dm_ref[...] = jnp.exp(jnp.abs(ii - jj).astype(F32) * jnp.where(ii >= jj, lgf, lgb))

    pos = lax.broadcasted_iota(jnp.int32, (ch, 128), 0).astype(F32)
    pd_ref[0] = jnp.exp((pos + 1.0) * lgf)
    pd_ref[1] = jnp.exp((ch - pos) * lgb)
    pd_ref[2] = jnp.exp((ch - 1.0 - pos) * lgf)
    pd_ref[3] = jnp.exp(pos * lgb)
    all_f = jnp.exp(ch * lgf)
    all_b = jnp.exp(ch * lgb)

    def rope_rows(c, _):
        r0 = pl.multiple_of(c * ch, ch)
        for s in (0, 128):
            cos_h = cos_ref[pl.ds(r0, ch), s:s + 128]
            sin_h = sin_ref[pl.ds(r0, ch), s:s + 128]
            qh = q_ref[pl.ds(r0, ch), s:s + 128].astype(F32)
            qr = (qh * cos_h + pltpu.roll(qh, 64, 1) * sin_h) * (RET_DK ** -0.5)
            qs_ref[pl.ds(r0, ch), s:s + 128] = qr.astype(BF16)
            qin_ref[pl.ds(r0, ch), s:s + 128] = (qr * pd_ref[0]).astype(BF16)
            qin_ref[pl.ds(r0, ch), dk + s:dk + s + 128] = (qr * pd_ref[1]).astype(BF16)
            kh = k_ref[pl.ds(r0, ch), s:s + 128].astype(F32)
            kr = kh * cos_h + pltpu.roll(kh, 64, 1) * sin_h
            ks_ref[pl.ds(r0, ch), s:s + 128] = kr.astype(BF16)
            kend_ref[pl.ds(r0, ch), s:s + 128] = (kr * pd_ref[2]).astype(BF16)
            kend_ref[pl.ds(r0, ch), dk + s:dk + s + 128] = (kr * pd_ref[3]).astype(BF16)
        return 0

    lax.fori_loop(0, n_ch, rope_rows, 0, unroll=math.gcd(RET_ROPE_UNROLL, n_ch))

    def outer(kb, vb):
        return lax.dot_general(kb, vb, TN_DIMS, preferred_element_type=F32)

    kc = kc_ref[...].astype(F32)
    end_f = jnp.concatenate([pd_ref[2], pd_ref[2]], axis=1)
    end_b = jnp.concatenate([pd_ref[3], pd_ref[3]], axis=1)
    sf_ref[...] = outer((kc * end_f).astype(BF16), vc_ref[...])
    sb_ref[...] = outer((kc * end_b).astype(BF16), vc_ref[...])

    def states(t, _):
        cf = t
        cb = n_ch - 1 - t
        rf = pl.multiple_of(cf * ch, ch)
        rb = pl.multiple_of(cb * ch, ch)
        vf = v_ref[pl.ds(rf, ch), :]
        rhs_ref[cf, 0:ch, :] = vf
        rhs_ref[cf, ch:ch + dk, :] = sf_ref[...].astype(BF16)
        rhs_ref[cb, ch + dk:ch + 2 * dk, :] = sb_ref[...].astype(BF16)
        sf_ref[...] = sf_ref[...] * all_f + outer(kend_ref[pl.ds(rf, ch), 0:dk], vf)
        sb_ref[...] = sb_ref[...] * all_b + outer(kend_ref[pl.ds(rb, ch), dk:2 * dk], v_ref[pl.ds(rb, ch), :])
        return 0

    lax.fori_loop(0, n_ch, states, 0, unroll=math.gcd(RET_STATE_UNROLL, n_ch))

    group = math.gcd(RET_GROUP, n_ch)

    def outputs(pair, _):
        cs_ = [group * pair + n for n in range(group)]
        starts = [pl.multiple_of(c * ch, ch) for c in cs_]
        scs = [lax.dot_general(qs_ref[pl.ds(r0, ch), :], ks_ref[pl.ds(r0, ch), :], NT_DIMS, preferred_element_type=F32)
               for r0 in starts]
        lhs = [jnp.concatenate([(sc * dm_ref[...]).astype(BF16), qin_ref[pl.ds(r0, ch), :]], axis=1)
               for sc, r0 in zip(scs, starts)]
        ys = [jnp.dot(l, rhs_ref[c], preferred_element_type=F32) for l, c in zip(lhs, cs_)]
        for r0, y in zip(starts, ys):
            mu = jnp.mean(y, axis=-1, keepdims=True)
            yc = y - mu
            var = jnp.mean(yc * yc, axis=-1, keepdims=True)
            g = g_ref[pl.ds(r0, ch), :].astype(F32)
            o_ref[pl.ds(r0, ch), :] = (yc * lax.rsqrt(var + EPS) * gnw_ref[...] * _silu(g)).astype(o_ref.dtype)
        return 0

    lax.fori_loop(0, n_ch // group, outputs, 0)


def _retention(proj_l, proj_c, cos_t, sin_t, dec, gnw):
    b, seq, _ = proj_l.shape
    ctx_len = proj_c.shape[1]
    assert ctx_len == RET_CHUNK and seq % (2 * RET_CHUNK) == 0
    n_ch = seq // RET_CHUNK
    kq, kv = RET_DK, RET_DV
    const = dict(pipeline_mode=pl.Buffered(1))
    return pl.pallas_call(
        _ret_kernel,
        out_shape=jax.ShapeDtypeStruct((b, seq, RET_V), BF16),
        grid=(b, RET_HEADS),
        in_specs=[pl.BlockSpec((None, seq, kq), lambda i, h: (i, 0, COL_Q // kq + h)),
                  pl.BlockSpec((None, seq, kq), lambda i, h: (i, 0, COL_K // kq + h)),
                  pl.BlockSpec((None, seq, kv), lambda i, h: (i, 0, COL_V // kv + h)),
                  pl.BlockSpec((None, seq, kv), lambda i, h: (i, 0, COL_G // kv + h)),
                  pl.BlockSpec((None, ctx_len, kq), lambda i, h: (i, 0, CTX_K // kq + h)),
                  pl.BlockSpec((None, ctx_len, kv), lambda i, h: (i, 0, CTX_V // kv + h)),
                  pl.BlockSpec((seq, kq), lambda i, h: (0, 0), **const),
                  pl.BlockSpec((seq, kq), lambda i, h: (0, 0), **const),
                  pl.BlockSpec((None, 2, 128), lambda i, h: (h, 0, 0)),
                  pl.BlockSpec((1, kv), lambda i, h: (0, h))],
        out_specs=pl.BlockSpec((None, seq, kv), lambda i, h: (i, 0, h)),
        scratch_shapes=[pltpu.VMEM((seq, kq), BF16), pltpu.VMEM((seq, 2 * kq), BF16),
                        pltpu.VMEM((seq, kq), BF16), pltpu.VMEM((seq, 2 * kq), BF16),
                        pltpu.VMEM((n_ch, RET_CHUNK + 2 * kq, kv), BF16),
                        pltpu.VMEM((kq, kv), F32), pltpu.VMEM((kq, kv), F32),
                        pltpu.VMEM((RET_CHUNK, RET_CHUNK), F32), pltpu.VMEM((4, RET_CHUNK, 128), F32)],
        compiler_params=pltpu.CompilerParams(dimension_semantics=("parallel", "arbitrary"), vmem_limit_bytes=VMEM_LIMIT),
        name="ret",
    )(proj_l, proj_l, proj_l, proj_l, proj_c, proj_c, cos_t, sin_t, dec, gnw)


def _ssd_kernel(xs_ref, b_ref, c_ref, z_ref, dt_ref, xsc_ref, bc_ref, dtc_ref,
                cwx_ref, cwb_ref, cwc_ref, cbx_ref, cbb_ref, cbc_ref, bias_ref, alog_ref, dsk_ref, nw_ref,
                shm_ref, tri_ref, emat_ref, hmask_ref, o_ref,
                pad_ref, xc_ref, xm_ref, bs_ref, cs_ref, dtv_ref, cum_ref, upd_ref, dec_ref, sall_ref,
                padc_ref, xcc_ref, bsc_ref, dtvc_ref, updc_ref, decc_ref, st_ref):
    seq = xs_ref.shape[0]
    ctx_len = xsc_ref.shape[0]
    q = SSD_CHUNK
    n_q = seq // q
    n_qc = ctx_len // q
    gw = SSD_GW
    ns = SSD_STATE
    ex = SSD_EXP
    tab = SSD_TAB
    halo = CONV_HALO

    lane_t = lax.broadcasted_iota(jnp.int32, (q, tab), 1)
    fwd_lane = (lane_t < gw) | ((lane_t >= ex) & (((lane_t - ex) & 7) < SSD_HPG))
    lane_c = lax.broadcasted_iota(jnp.int32, (q, 128), 1)
    piece = lane_c >> 3

    bias = bias_ref[...]
    neg_a = -jnp.exp(alog_ref[...]) * LOG2_E
    lane_h = lax.broadcasted_iota(jnp.int32, (q, 128), 1)

    def prep(pad, srcs, weights, biases, dts, nrows, dsts, dtv_dst, with_xm):
        width = pad.shape[1]
        pad[0:halo, :] = jnp.zeros((halo, width), BF16)
        pad[nrows + halo:nrows + 2 * halo, :] = jnp.zeros((halo, width), BF16)
        off = 0
        for s in srcs:
            w_ = s.shape[1]
            pad[halo:nrows + halo, off:off + w_] = s[...]
            off += w_
        cw = jnp.concatenate([w[...] for w in weights], axis=1)
        cb = jnp.concatenate([b_[...] for b_ in biases], axis=1)
        rows = CONV_ROWS
        half_w = SSD_CONV_W // 2
        strip_dst = []
        for dst in dsts:
            strip_dst += [(dst, o) for o in range(0, dst.shape[1], 128)]
        sub8 = lax.broadcasted_iota(jnp.int32, (8, 128), 0)

        def matmul_stage(i):
            r0 = pl.multiple_of(i * rows, rows)
            blk = pad[pl.ds(r0 + halo, rows), :]
            sh = jnp.dot(shm_ref[...], blk, preferred_element_type=F32)
            hi, lo = _split_bf16(_softplus(dts[pl.ds(r0, rows), :] + bias), 2)
            dtv_dst[pl.ds(r0, rows), :] = jnp.dot(jnp.concatenate([hi, lo], axis=1).astype(BF16), emat_ref[...],
                                                  preferred_element_type=F32)
            return r0, blk, sh

        def tap_stage(i, r0, blk, sh):
            prev = pad[pl.ds(r0, halo), :].astype(F32)[halo - 8:halo, :]
            nxt = pad[pl.ds(r0 + halo + rows, halo), :].astype(F32)[0:8, :]
            for s_, (dst, o) in enumerate(strip_dst):
                lo_, hi_ = s_ * 128, (s_ + 1) * 128
                acc = cb[:, lo_:hi_] + cw[half_w:half_w + 1, lo_:hi_] * blk[:, lo_:hi_].astype(F32)
                for n, off_ in enumerate(CONV_SHIFTS):
                    shf = sh[n * rows:(n + 1) * rows, lo_:hi_]
                    if off_ < 0:
                        edge = shf[0:8, :]
                        for e in range(-off_):
                            edge = jnp.where(sub8 == e, prev[8 + e + off_:9 + e + off_, lo_:hi_], edge)
                        shf = jnp.concatenate([edge, shf[8:, :]], axis=0)
                    else:
                        edge = shf[rows - 8:rows, :]
                        for e in range(off_):
                            edge = jnp.where(sub8 == 7 - e, nxt[off_ - 1 - e:off_ - e, lo_:hi_], edge)
                        shf = jnp.concatenate([shf[:rows - 8, :], edge], axis=0)
                    acc = acc + cw[off_ + half_w:off_ + half_w + 1, lo_:hi_] * shf
                y = _silu(acc)
                dst[pl.ds(r0, rows), o:o + 128] = y.astype(dst.dtype)
                if with_xm and dst is xc_ref:
                    for half in range(rows // q):
                        yh = y[half * q:(half + 1) * q, :]
                        for r in range(SSD_HPG):
                            row = pl.multiple_of(i * (rows // q) * SSD_HPG * q, q) + (half * SSD_HPG + r) * q
                            if r // 2 == s_:
                                keep = (lane_h < SSD_HEADDIM) if r % 2 == 0 else (lane_h >= SSD_HEADDIM)
                                xm_ref[pl.ds(row, q), o:o + 128] = jnp.where(keep, yh, 0.0).astype(BF16)
                            else:
                                xm_ref[pl.ds(row, q), o:o + 128] = jnp.zeros((q, 128), BF16)

        group = math.gcd(CONV_GROUP, nrows // rows)

        def conv(t, _):
            ids = [group * t + n for n in range(group)]
            staged = [matmul_stage(i) for i in ids]
            for i, st in zip(ids, staged):
                tap_stage(i, *st)
            return 0

        lax.fori_loop(0, nrows // rows // group, conv, 0)

    prep(pad_ref, (xs_ref, b_ref, c_ref), (cwx_ref, cwb_ref, cwc_ref), (cbx_ref, cbb_ref, cbc_ref),
         dt_ref, seq, (xc_ref, bs_ref, cs_ref), dtv_ref, True)
    prep(padc_ref, (xsc_ref, bc_ref), (cwx_ref, cwb_ref), (cbx_ref, cbb_ref),
         dtc_ref, ctx_len, (xcc_ref, bsc_ref), dtvc_ref, False)

    def tables(dtvs):
        a_s = [dtv * neg_a for dtv in dtvs]
        his, los = zip(*[_split_bf16(a, 2) for a in a_s])
        rhs = jnp.concatenate([jnp.concatenate(his, axis=1), jnp.concatenate(los, axis=1)], axis=0).astype(BF16)
        p_all = jnp.dot(tri_ref[...], rhs, preferred_element_type=F32)
        out = []
        for n, a in enumerate(a_s):
            p = p_all[:, n * tab:(n + 1) * tab]
            tot = p[q - 1:q, :]
            out.append((jnp.where(fwd_lane, p, tot - p + a), tot))
        return out

    def weighted_x(xcf, dtv, cum, tot):
        coef = dtv[:, 0:ex] * jnp.exp2(tot[:, 0:ex] - cum[:, 0:ex])
        return (jnp.concatenate([xcf, xcf], axis=1) * coef).astype(BF16)

    def contribution(bb, xw):
        return lax.dot_general(bb, xw, TN_DIMS, preferred_element_type=F32)

    def pass_a_pair(starts, load, cum_dst, upd_dst, dec_dst, chunk_ids):
        xcs, bbs, dtvs = zip(*[load(r0) for r0 in starts])
        tabs = tables(dtvs)
        xws = [weighted_x(xcf, dtv, cum, tot) for xcf, dtv, (cum, tot) in zip(xcs, dtvs, tabs)]
        for n, (r0, c) in enumerate(zip(starts, chunk_ids)):
            cum, tot = tabs[n]
            if cum_dst is not None:
                cum_dst[pl.ds(r0, q), :] = cum
            upd_dst[c] = contribution(bbs[n], xws[n])
            d0 = c * 8 if isinstance(c, int) else pl.multiple_of(c * 8, 8)
            dec_dst[pl.ds(d0, 8), :] = jnp.broadcast_to(jnp.exp2(tot[:, 0:ex]), (8, ex))

    assert n_qc == 2
    group_a = math.gcd(SSD_GROUP_A, n_q)
    group_c = math.gcd(SSD_GROUP_C, n_q)
    pass_a_pair([0, q], lambda r0: (xcc_ref[r0:r0 + q, :], bsc_ref[r0:r0 + q, :], dtvc_ref[r0:r0 + q, :]),
                None, updc_ref, decc_ref, [0, 1])

    def pass_a(grp_i, _):
        ids = [group_a * grp_i + n for n in range(group_a)]
        starts = [pl.multiple_of(c * q, q) for c in ids]
        pass_a_pair(starts, lambda r0: (xc_ref[pl.ds(r0, q), :], bs_ref[pl.ds(r0, q), :], dtv_ref[pl.ds(r0, q), :]),
                    cum_ref, upd_ref, dec_ref, ids)
        return 0

    lax.fori_loop(0, n_q // group_a, pass_a, 0)

    st_ref[...] = jnp.zeros((ns, ex), F32)
    for cc in range(n_qc):
        st_ref[:, 0:gw] = st_ref[:, 0:gw] * decc_ref[cc * 8:cc * 8 + 1, 0:gw] + updc_ref[cc, :, 0:gw]
    for cc in reversed(range(n_qc)):
        st_ref[:, gw:ex] = st_ref[:, gw:ex] * decc_ref[cc * 8:cc * 8 + 1, gw:ex] + updc_ref[cc, :, gw:ex]

    def pass_b(t, _):
        cf = t
        cb = n_q - 1 - t
        sall_ref[cf, :, 0:gw] = st_ref[:, 0:gw].astype(BF16)
        sall_ref[cb, :, gw:ex] = st_ref[:, gw:ex].astype(BF16)
        df = dec_ref[pl.ds(pl.multiple_of(cf * 8, 8), 8), :]
        db = dec_ref[pl.ds(pl.multiple_of(cb * 8, 8), 8), :]
        st_ref[:, 0:gw] = st_ref[:, 0:gw] * df[0:1, 0:gw] + upd_ref[cf, :, 0:gw]
        st_ref[:, gw:ex] = st_ref[:, gw:ex] * db[0:1, gw:ex] + upd_ref[cb, :, gw:ex]
        return 0

    lax.fori_loop(0, n_q, pass_b, 0)

    ii = lax.broadcasted_iota(jnp.int32, (q, q), 0)
    jj = lax.broadcasted_iota(jnp.int32, (q, q), 1)
    dsk = dsk_ref[...]
    nw = nw_ref[...]

    def exponents(r0):
        cum_c = cum_ref[pl.ds(r0, q), ex:tab]
        dt_c = dtv_ref[pl.ds(r0, q), ex:tab]
        ldt = jnp.where(dt_c > 0.0, jnp.log2(dt_c), NEG_BIG)
        c0, c1 = _split_bf16(cum_c, 2)
        l0, l1 = _split_bf16(ldt, 2)
        u = jnp.where(piece == 0, c0, jnp.where(piece == 1, c1, jnp.where(piece < 6, 1.0, 0.0)))
        v = jnp.where(piece < 2, 1.0, jnp.where(piece == 2, -c0, jnp.where(piece == 3, -c1, jnp.where(
            piece == 4, l0, jnp.where(piece == 5, l1, 0.0)))))
        vb = v.astype(BF16)
        vst = jnp.concatenate([vb * hmask_ref[m] for m in range(2 * SSD_HPG)], axis=0)
        return lax.dot_general(u.astype(BF16), vst, NT_DIMS, preferred_element_type=F32)

    def intra_weights(s, arg_all):
        lower = ii >= jj
        ws = []
        for r in range(SSD_HPG):
            arg = jnp.where(lower, arg_all[:, r * q:(r + 1) * q], arg_all[:, (SSD_HPG + r) * q:(SSD_HPG + r + 1) * q])
            ws.append((s * jnp.exp2(arg)).astype(BF16))
        return jnp.concatenate(ws, axis=1)

    def finish(r0, y, yi):
        ecum = jnp.exp2(cum_ref[pl.ds(r0, q), 0:ex])
        y = y + yi[:, 0:gw] * ecum[:, 0:gw] + yi[:, gw:ex] * ecum[:, gw:ex]
        y = (y + dsk * xc_ref[pl.ds(r0, q), :]) * _silu(z_ref[pl.ds(r0, q), :].astype(F32))
        y = y * lax.rsqrt(jnp.mean(y * y, axis=-1, keepdims=True) + EPS) * nw
        o_ref[pl.ds(r0, q), :] = y.astype(o_ref.dtype)

    def pass_c(grp_i, _):
        cs_ = [group_c * grp_i + n for n in range(group_c)]
        starts = [pl.multiple_of(c * q, q) for c in cs_]
        args = [exponents(r0) for r0 in starts]
        cbs = [cs_ref[pl.ds(r0, q), :] for r0 in starts]
        scores = [lax.dot_general(cb, bs_ref[pl.ds(r0, q), :], NT_DIMS, preferred_element_type=F32)
                  for cb, r0 in zip(cbs, starts)]
        yis = [jnp.dot(cb, sall_ref[c], preferred_element_type=F32) for cb, c in zip(cbs, cs_)]
        wcat = [intra_weights(s, a) for s, a in zip(scores, args)]
        ys = [jnp.dot(w, xm_ref[pl.ds(pl.multiple_of(c * SSD_HPG * q, q), SSD_HPG * q), :], preferred_element_type=F32)
              for w, c in zip(wcat, cs_)]
        for r0, y, yi in zip(starts, ys, yis):
            finish(r0, y, yi)
        return 0

    lax.fori_loop(0, n_q // group_c, pass_c, 0)


def _ssd_constants():
    rows, q, tab, ex = CONV_ROWS, SSD_CHUNK, SSD_TAB, SSD_EXP
    t = np.arange(rows)
    shm = np.concatenate([(t[None, :] == t[:, None] + o) for o in CONV_SHIFTS], axis=0)
    tq = np.arange(q)
    tri = np.tile(tq[None, :] <= tq[:, None], (1, 2))
    lane = np.arange(tab)
    want = np.where(lane < ex, 4 * (lane // SSD_GW) + (lane % SSD_GW) // SSD_HEADDIM,
                    np.where(lane < ex + 64, (lane - ex) % 8, -1000))
    src = np.arange(2 * DT_LANES) % DT_LANES
    emat = np.stack([(src[:, None] - 8 * g) == want[None, :] for g in range(SSD_GROUPS)], axis=0)
    as_bf16 = lambda a: jnp.asarray(a.astype(np.float32), dtype=BF16)
    hmask = np.broadcast_to((np.arange(128) % 8)[None, None, :] == np.arange(2 * SSD_HPG)[:, None, None],
                            (2 * SSD_HPG, q, 128))
    return as_bf16(shm), as_bf16(tri), as_bf16(emat), as_bf16(hmask)


def _ssd(proj_l, proj_c, dt_l, dt_c, conv_w, conv_b, bias, alog, dsk, nw):
    b, seq, _ = proj_l.shape
    ctx_len = proj_c.shape[1]
    assert seq % CONV_ROWS == 0 and ctx_len % CONV_ROWS == 0 and CONV_ROWS % SSD_CHUNK == 0
    gw, ns, q, ex, tab = SSD_GW, SSD_STATE, SSD_CHUNK, SSD_EXP, SSD_TAB
    n_q, n_qc = seq // q, ctx_len // q
    cx, cb_, cc_ = 0, SSD_INNER // ns, (SSD_INNER + SSD_BC) // ns
    shm, tri, emat, hmask = _ssd_constants()
    return pl.pallas_call(
        _ssd_kernel,
        out_shape=jax.ShapeDtypeStruct((b, seq, SSD_INNER), BF16),
        grid=(b, SSD_GROUPS),
        in_specs=[pl.BlockSpec((None, seq, gw), lambda i, g: (i, 0, COL_XS // gw + g)),
                  pl.BlockSpec((None, seq, ns), lambda i, g: (i, 0, COL_B // ns + g)),
                  pl.BlockSpec((None, seq, ns), lambda i, g: (i, 0, COL_C // ns + g)),
                  pl.BlockSpec((None, seq, gw), lambda i, g: (i, 0, COL_Z // gw + g)),
                  pl.BlockSpec((None, seq, DT_LANES), lambda i, g: (i, 0, 0)),
                  pl.BlockSpec((None, ctx_len, gw), lambda i, g: (i, 0, CTX_XS // gw + g)),
                  pl.BlockSpec((None, ctx_len, ns), lambda i, g: (i, 0, CTX_B // ns + g)),
                  pl.BlockSpec((None, ctx_len, DT_LANES), lambda i, g: (i, 0, 0)),
                  pl.BlockSpec((SSD_CONV_W, gw), lambda i, g: (0, cx + g)),
                  pl.BlockSpec((SSD_CONV_W, ns), lambda i, g: (0, cb_ + g)),
                  pl.BlockSpec((SSD_CONV_W, ns), lambda i, g: (0, cc_ + g)),
                  pl.BlockSpec((1, gw), lambda i, g: (0, cx + g)),
                  pl.BlockSpec((1, ns), lambda i, g: (0, cb_ + g)),
                  pl.BlockSpec((1, ns), lambda i, g: (0, cc_ + g)),
                  pl.BlockSpec((1, DT_LANES), lambda i, g: (0, 0)),
                  pl.BlockSpec((None, 1, tab), lambda i, g: (g, 0, 0)),
                  pl.BlockSpec((1, gw), lambda i, g: (0, g)),
                  pl.BlockSpec((1, gw), lambda i, g: (0, g)),
                  pl.BlockSpec(shm.shape, lambda i, g: (0, 0)),
                  pl.BlockSpec(tri.shape, lambda i, g: (0, 0)),
                  pl.BlockSpec((None,) + emat.shape[1:], lambda i, g: (g, 0, 0)),
                  pl.BlockSpec(hmask.shape, lambda i, g: (0, 0, 0))],
        out_specs=pl.BlockSpec((None, seq, gw), lambda i, g: (i, 0, g)),
        scratch_shapes=[pltpu.VMEM((seq + 2 * CONV_HALO, gw + 2 * ns), BF16),
                        pltpu.VMEM((seq, gw), F32),
                        pltpu.VMEM((n_q * SSD_HPG * q, gw), BF16),
                        pltpu.VMEM((seq, ns), BF16), pltpu.VMEM((seq, ns), BF16),
                        pltpu.VMEM((seq, tab), F32), pltpu.VMEM((seq, tab), F32),
                        pltpu.VMEM((n_q, ns, ex), F32), pltpu.VMEM((n_q * 8, ex), F32),
                        pltpu.VMEM((n_q, ns, ex), BF16),
                        pltpu.VMEM((ctx_len + 2 * CONV_HALO, gw + ns), BF16),
                        pltpu.VMEM((ctx_len, gw), F32), pltpu.VMEM((ctx_len, ns), BF16),
                        pltpu.VMEM((ctx_len, tab), F32),
                        pltpu.VMEM((n_qc, ns, ex), F32), pltpu.VMEM((n_qc * 8, ex), F32),
                        pltpu.VMEM((ns, ex), F32)],
        compiler_params=pltpu.CompilerParams(dimension_semantics=("parallel", "arbitrary"), vmem_limit_bytes=VMEM_LIMIT),
        name="ssd",
    )(proj_l, proj_l, proj_l, proj_l, dt_l, proj_c, proj_c, dt_c,
      conv_w, conv_w, conv_w, conv_b, conv_b, conv_b, bias, alog, dsk, nw, shm, tri, emat, hmask)


def _merge_kernel(yr_ref, ys_ref, gt_ref, x_ref, gate_ref, wr_ref, ws_ref, wo_ref, nw_ref, o_ref):
    o_r = jnp.dot(yr_ref[...], wr_ref[...], preferred_element_type=F32)
    o_s = jnp.dot(ys_ref[...], ws_ref[...], preferred_element_type=F32)
    gt = gt_ref[...].astype(F32)
    m = jax.nn.sigmoid(gt[:, :D_MODEL]) * o_r + jax.nn.sigmoid(gt[:, D_MODEL:]) * o_s
    out = jnp.dot(m.astype(BF16), wo_ref[...], preferred_element_type=F32)
    nrm = out * lax.rsqrt(jnp.mean(out * out, axis=-1, keepdims=True) + EPS) * nw_ref[...]
    o_ref[...] = x_ref[...] + gate_ref[...] * nrm


def _merge(yr, ys, proj_l, x2, gate, w_ret_o, w_ssd_o, w_out, nw, seq_len):
    m = x2.shape[0]
    tm = min(512, seq_len)
    per = seq_len // tm
    const = dict(pipeline_mode=pl.Buffered(1))
    return pl.pallas_call(
        _merge_kernel,
        out_shape=jax.ShapeDtypeStruct((m, D_MODEL), F32),
        grid=(m // tm,),
        in_specs=[pl.BlockSpec((tm, RET_V), lambda i: (i, 0)),
                  pl.BlockSpec((tm, SSD_INNER), lambda i: (i, 0)),
                  pl.BlockSpec((tm, 2 * D_MODEL), lambda i: (i, COL_GATES // (2 * D_MODEL))),
                  pl.BlockSpec((tm, D_MODEL), lambda i: (i, 0)),
                  pl.BlockSpec((None, 1, D_MODEL), lambda i: (i // per, 0, 0)),
                  pl.BlockSpec((RET_V, D_MODEL), lambda i: (0, 0), **const),
                  pl.BlockSpec((SSD_INNER, D_MODEL), lambda i: (0, 0), **const),
                  pl.BlockSpec((D_MODEL, D_MODEL), lambda i: (0, 0), **const),
                  pl.BlockSpec((1, D_MODEL), lambda i: (0, 0))],
        out_specs=pl.BlockSpec((tm, D_MODEL), lambda i: (i, 0)),
        compiler_params=pltpu.CompilerParams(dimension_semantics=("parallel",), vmem_limit_bytes=VMEM_LIMIT),
        name="merge",
    )(yr, ys, proj_l, x2, gate, w_ret_o, w_ssd_o, w_out, nw)


def _rope_tables(n_tokens):
    pos = jnp.arange(n_tokens)
    row = (pos // GRID_W).astype(F32)
    col = (pos % GRID_W).astype(F32)
    inv_freq = ROPE_THETA ** (-jnp.arange(ROPE_FREQS, dtype=F32) / ROPE_FREQS)
    ar = row[:, None] * inv_freq
    ac = col[:, None] * inv_freq
    cos_t = jnp.concatenate([jnp.cos(ar), jnp.cos(ar), jnp.cos(ac), jnp.cos(ac)], axis=1)
    sin_t = jnp.concatenate([-jnp.sin(ar), jnp.sin(ar), -jnp.sin(ac), jnp.sin(ac)], axis=1)
    return cos_t, sin_t


def _group_major(t):
    lead = t.shape[:-2]
    t = t.reshape(lead + (2, SSD_GROUPS, SSD_HPG))
    t = jnp.moveaxis(t, -3, -2)
    return t.reshape(lead + (SSD_GROUPS, 2 * SSD_HPG))


def _table_lanes(t):
    gm = _group_major(t)
    expanded = jnp.repeat(gm, SSD_HEADDIM, axis=1)
    compact = jnp.pad(jnp.tile(gm, (1, 8)), ((0, 0), (0, 64)))
    return jnp.concatenate([expanded, compact], axis=1)[:, None, :]


def kernel(x, c, ctx, c_ctx, w_mod, b_mod, norm_pre_w, norm_post_w, w_in, ret_decay, ret_gn_w, ssd_conv_w, ssd_conv_b,
           ssd_dt_bias, ssd_a_log, ssd_D, ssd_norm_w, w_ret_o, w_ssd_o, w_out):
    assert w_in.shape[0] == 1, "single-layer problem: the context stream is never updated"
    b, seq, d = x.shape
    ctx_len = ctx.shape[1]

    offs = [0]
    for s in IN_SIZES:
        offs.append(offs[-1] + s)
    dt_lo, dt_hi = offs[6], offs[7]
    w_main = jnp.concatenate([w_in[0][:, :dt_lo], w_in[0][:, dt_hi:]], axis=1).astype(BF16)
    wdt = _group_major(w_in[0][:, dt_lo:dt_hi].reshape(d, 2, SSD_HEADS)).reshape(d, 2 * SSD_HEADS)
    wdt = jnp.pad(wdt, ((0, 0), (0, DT_LANES - 2 * SSD_HEADS))).astype(BF16)
    bias = jnp.pad(_group_major(ssd_dt_bias[0]).reshape(1, 2 * SSD_HEADS), ((0, 0), (0, DT_LANES - 2 * SSD_HEADS)))
    alog = _table_lanes(ssd_a_log[0])
    dsk = jnp.repeat(ssd_D[0], SSD_HEADDIM)[None, :]
    dec = jnp.broadcast_to(ret_decay[0].T[:, :, None], (RET_HEADS, 2, 128))

    n_rows = -(-(b + 1) // 8) * 8
    c_all = jnp.concatenate([c, c_ctx[None, :], jnp.zeros((n_rows - b - 1, d), F32)], axis=0)
    mod = _modulation(c_all, w_mod[0], b_mod[0][None, :])
    shift, scale, gate = mod[:, :d], mod[:, d:2 * d], mod[:, 2 * d:]
    nw_pre = norm_pre_w[0][None, :]

    x2 = x.reshape(b * seq, d)
    proj_l, dt_l = _inproj(x2, shift[:b, None, :], scale[:b, None, :], nw_pre, w_main, wdt, seq, N_MAIN, "inproj_latent")
    proj_c, dt_c = _inproj(ctx.reshape(b * ctx_len, d), shift[b:b + 1, None, :], scale[b:b + 1, None, :], nw_pre,
                           w_main, wdt, ctx_len, N_CTX, "inproj_ctx")
    proj_l = proj_l.reshape(b, seq, N_MAIN)
    proj_c = proj_c.reshape(b, ctx_len, N_CTX)
    dt_l = dt_l.reshape(b, seq, DT_LANES)
    dt_c = dt_c.reshape(b, ctx_len, DT_LANES)

    cos_t, sin_t = _rope_tables(seq)
    yr = _retention(proj_l, proj_c, cos_t, sin_t, dec, ret_gn_w[0][None, :])
    ys = _ssd(proj_l, proj_c, dt_l, dt_c, ssd_conv_w[0], ssd_conv_b[0][None, :], bias, alog, dsk, ssd_norm_w[0][None, :])

    out = _merge(yr.reshape(b * seq, RET_V), ys.reshape(b * seq, SSD_INNER), proj_l.reshape(b * seq, N_MAIN), x2,
                 gate[:b, None, :], w_ret_o[0].astype(BF16), w_ssd_o[0].astype(BF16), w_out[0].astype(BF16),
                 norm_post_w[0][None, :], seq)
    return out.reshape(b, seq, d)
```

```python
import math

import jax
import jax.numpy as jnp
import numpy as np
from jax import lax
from jax.experimental import pallas as pl
from jax.experimental.pallas import tpu as pltpu

F32 = jnp.float32
BF16 = jnp.bfloat16
HIGHEST = lax.Precision.HIGHEST

D_MODEL = 1024
EPS = 1e-6
GRID_W = 64
ROPE_THETA = 10000.0
ROPE_FREQS = 64

RET_HEADS = 4
RET_DK = 256
RET_DV = 512
RET_QK = RET_HEADS * RET_DK
RET_V = RET_HEADS * RET_DV
RET_CHUNK = 256
RET_GROUP = 8
RET_ROPE_UNROLL = 2
RET_STATE_UNROLL = 8

SSD_INNER = 2048
SSD_HEADDIM = 64
SSD_HEADS = 32
SSD_GROUPS = 8
SSD_HPG = 4
SSD_STATE = 128
SSD_BC = SSD_GROUPS * SSD_STATE
SSD_CONV_W = 5
SSD_GW = SSD_HPG * SSD_HEADDIM
SSD_CHUNK = 128
SSD_GROUP_A = 16
SSD_GROUP_C = 16
CONV_ROWS = 256
CONV_GROUP = 8
CONV_HALO = 16
CONV_SHIFTS = tuple(o for o in range(-(SSD_CONV_W // 2), SSD_CONV_W // 2 + 1) if o != 0)
DT_LANES = 128
SSD_EXP = 2 * SSD_GW
SSD_TAB = SSD_EXP + 128
NEG_BIG = -1e30
LOG2_E = 1.4426950408889634

IN_SIZES = (RET_QK, RET_QK, RET_V, RET_V, SSD_INNER, SSD_INNER + 2 * SSD_BC, 2 * SSD_HEADS, 2 * D_MODEL)

COL_Q = 0
COL_K = COL_Q + RET_QK
COL_V = COL_K + RET_QK
COL_G = COL_V + RET_V
COL_Z = COL_G + RET_V
COL_XS = COL_Z + SSD_INNER
COL_B = COL_XS + SSD_INNER
COL_C = COL_B + SSD_BC
COL_GATES = COL_C + SSD_BC
N_MAIN = COL_GATES + 2 * D_MODEL
CTX_K = 0
CTX_V = CTX_K + RET_QK
CTX_XS = CTX_V + RET_V
CTX_B = CTX_XS + SSD_INNER
CTX_C = CTX_B + SSD_BC
N_CTX = CTX_C + SSD_BC
INPROJ_TN = 1024
CTX_TILES_KV = (CTX_XS - CTX_K) // INPROJ_TN
CTX_OFF_KV = COL_K // INPROJ_TN
CTX_OFF_XBC = (COL_XS - CTX_XS) // INPROJ_TN

VMEM_LIMIT = 48 * 1024 * 1024

NT_DIMS = (((1,), (1,)), ((), ()))
TN_DIMS = (((0,), (0,)), ((), ()))


def _silu(x):
    return x * jax.nn.sigmoid(x)


def _softplus(x):
    return jnp.maximum(x, 0.0) + jnp.log1p(jnp.exp(-jnp.abs(x)))


def _log_sigmoid(x):
    return jnp.minimum(x, 0.0) - jnp.log1p(jnp.exp(-jnp.abs(x)))


def _split_bf16(a, parts):
    out = []
    for _ in range(parts):
        p = a.astype(BF16).astype(F32)
        out.append(p)
        a = a - p
    return out


def _mod_kernel(c_ref, w_ref, b_ref, o_ref):
    o_ref[...] = jnp.dot(_silu(c_ref[...]), w_ref[...], preferred_element_type=F32, precision=HIGHEST) + b_ref[...]


def _modulation(c_all, w_mod, b_mod):
    rows = c_all.shape[0]
    n = w_mod.shape[1]
    tn = D_MODEL
    return pl.pallas_call(
        _mod_kernel,
        out_shape=jax.ShapeDtypeStruct((rows, n), F32),
        grid=(n // tn,),
        in_specs=[pl.BlockSpec((rows, D_MODEL), lambda j: (0, 0)),
                  pl.BlockSpec((D_MODEL, tn), lambda j: (0, j)),
                  pl.BlockSpec((1, tn), lambda j: (0, j))],
        out_specs=pl.BlockSpec((rows, tn), lambda j: (0, j)),
        compiler_params=pltpu.CompilerParams(dimension_semantics=("arbitrary",), vmem_limit_bytes=VMEM_LIMIT),
        name="mod",
    )(c_all, w_mod, b_mod)


def _inproj_kernel(x_ref, shift_ref, scale_ref, nw_ref, w_ref, wdt_ref, o_ref, dt_ref, h_ref):
    @pl.when(pl.program_id(1) == 0)
    def _():
        x = x_ref[...]
        r = lax.rsqrt(jnp.mean(x * x, axis=-1, keepdims=True) + EPS)
        h = (x * r * nw_ref[...] * (1.0 + scale_ref[...]) + shift_ref[...]).astype(BF16)
        h_ref[...] = h
        dt_ref[...] = jnp.dot(h, wdt_ref[...], preferred_element_type=F32)

    o_ref[...] = jnp.dot(h_ref[...], w_ref[...], preferred_element_type=F32).astype(o_ref.dtype)


def _inproj(x2, shift, scale, nw, w, wdt, seq_len, n_cols, name):
    m = x2.shape[0]
    nb = shift.shape[0]
    tm = min(2048, seq_len if nb > 1 else m)
    tn = INPROJ_TN
    per = seq_len // tm
    mod_idx = (lambda i, j: (i // per, 0, 0)) if nb > 1 else (lambda i, j: (0, 0, 0))
    if n_cols == N_MAIN:
        w_idx = lambda i, j: (0, j)
    else:
        w_idx = lambda i, j: (0, jnp.where(j < CTX_TILES_KV, j + CTX_OFF_KV, j + CTX_OFF_XBC))
    return pl.pallas_call(
        _inproj_kernel,
        out_shape=(jax.ShapeDtypeStruct((m, n_cols), BF16), jax.ShapeDtypeStruct((m, DT_LANES), F32)),
        grid=(m // tm, n_cols // tn),
        in_specs=[pl.BlockSpec((tm, D_MODEL), lambda i, j: (i, 0)),
                  pl.BlockSpec((None, 1, D_MODEL), mod_idx),
                  pl.BlockSpec((None, 1, D_MODEL), mod_idx),
                  pl.BlockSpec((1, D_MODEL), lambda i, j: (0, 0)),
                  pl.BlockSpec((D_MODEL, tn), w_idx),
                  pl.BlockSpec((D_MODEL, DT_LANES), lambda i, j: (0, 0))],
        out_specs=(pl.BlockSpec((tm, tn), lambda i, j: (i, j)),
                   pl.BlockSpec((tm, DT_LANES), lambda i, j: (i, 0))),
        scratch_shapes=[pltpu.VMEM((tm, D_MODEL), BF16)],
        compiler_params=pltpu.CompilerParams(dimension_semantics=("parallel", "arbitrary"), vmem_limit_bytes=VMEM_LIMIT),
        name=name,
    )(x2, shift, scale, nw, w, wdt)


def _ret_kernel(q_ref, k_ref, v_ref, g_ref, kc_ref, vc_ref, cos_ref, sin_ref, dec_ref, gnw_ref, o_ref,
                qs_ref, qin_ref, ks_ref, kend_ref, rhs_ref, sf_ref, sb_ref, dm_ref, pd_ref):
    seq = q_ref.shape[0]
    ch = RET_CHUNK
    n_ch = seq // ch
    dk = RET_DK

    lg = _log_sigmoid(dec_ref[...])
    lgf = lg[0:1, 0:1]
    lgb = lg[1:2, 0:1]

    ii = lax.broadcasted_iota(jnp.int32, (ch, ch), 0)
    jj = lax.broadcasted_iota(jnp.int32, (ch, ch), 1)
    dm_ref[...] = jnp.exp(jnp.abs(ii - jj).astype(F32) * jnp.where(ii >= jj, lgf, lgb))

    pos = lax.broadcasted_iota(jnp.int32, (ch, 128), 0).astype(F32)
    pd_ref[0] = jnp.exp((pos + 1.0) * lgf)
    pd_ref[1] = jnp.exp((ch - pos) * lgb)
    pd_ref[2] = jnp.exp((ch - 1.0 - pos) * lgf)
    pd_ref[3] = jnp.exp(pos * lgb)
    all_f = jnp.exp(ch * lgf)
    all_b = jnp.exp(ch * lgb)

    def rope_rows(c, _):
        r0 = pl.multiple_of(c * ch, ch)
        for s in (0, 128):
            cos_h = cos_ref[pl.ds(r0, ch), s:s + 128]
            sin_h = sin_ref[pl.ds(r0, ch), s:s + 128]
            qh = q_ref[pl.ds(r0, ch), s:s + 128].astype(F32)
            qr = (qh * cos_h + pltpu.roll(qh, 64, 1) * sin_h) * (RET_DK ** -0.5)
            qs_ref[pl.ds(r0, ch), s:s + 128] = qr.astype(BF16)
            qin_ref[pl.ds(r0, ch), s:s + 128] = (qr * pd_ref[0]).astype(BF16)
            qin_ref[pl.ds(r0, ch), dk + s:dk + s + 128] = (qr * pd_ref[1]).astype(BF16)
            kh = k_ref[pl.ds(r0, ch), s:s + 128].astype(F32)
            kr = kh * cos_h + pltpu.roll(kh, 64, 1) * sin_h
            ks_ref[pl.ds(r0, ch), s:s + 128] = kr.astype(BF16)
            kend_ref[pl.ds(r0, ch), s:s + 128] = (kr * pd_ref[2]).astype(BF16)
            kend_ref[pl.ds(r0, ch), dk + s:dk + s + 128] = (kr * pd_ref[3]).astype(BF16)
        return 0

    lax.fori_loop(0, n_ch, rope_rows, 0, unroll=math.gcd(RET_ROPE_UNROLL, n_ch))

    def outer(kb, vb):
        return lax.dot_general(kb, vb, TN_DIMS, preferred_element_type=F32)

    kc = kc_ref[...].astype(F32)
    end_f = jnp.concatenate([pd_ref[2], pd_ref[2]], axis=1)
    end_b = jnp.concatenate([pd_ref[3], pd_ref[3]], axis=1)
    sf_ref[...] = outer((kc * end_f).astype(BF16), vc_ref[...])
    sb_ref[...] = outer((kc * end_b).astype(BF16), vc_ref[...])

    def states(t, _):
        cf = t
        cb = n_ch - 1 - t
        rf = pl.multiple_of(cf * ch, ch)
        rb = pl.multiple_of(cb * ch, ch)
        vf = v_ref[pl.ds(rf, ch), :]
        rhs_ref[cf, 0:ch, :] = vf
        rhs_ref[cf, ch:ch + dk, :] = sf_ref[...].astype(BF16)
        rhs_ref[cb, ch + dk:ch + 2 * dk, :] = sb_ref[...].astype(BF16)
        sf_ref[...] = sf_ref[...] * all_f + outer(kend_ref[pl.ds(rf, ch), 0:dk], vf)
        sb_ref[...] = sb_ref[...] * all_b + outer(kend_ref[pl.ds(rb, ch), dk:2 * dk], v_ref[pl.ds(rb, ch), :])
        return 0

    lax.fori_loop(0, n_ch, states, 0, unroll=math.gcd(RET_STATE_UNROLL, n_ch))

    group = math.gcd(RET_GROUP, n_ch)

    def outputs(pair, _):
        cs_ = [group * pair + n for n in range(group)]
        starts = [pl.multiple_of(c * ch, ch) for c in cs_]
        scs = [lax.dot_general(qs_ref[pl.ds(r0, ch), :], ks_ref[pl.ds(r0, ch), :], NT_DIMS, preferred_element_type=F32)
               for r0 in starts]
        lhs = [jnp.concatenate([(sc * dm_ref[...]).astype(BF16), qin_ref[pl.ds(r0, ch), :]], axis=1)
               for sc, r0 in zip(scs, starts)]
        ys = [jnp.dot(l, rhs_ref[c], preferred_element_type=F32) for l, c in zip(lhs, cs_)]
        for r0, y in zip(starts, ys):
            mu = jnp.mean(y, axis=-1, keepdims=True)
            yc = y - mu
            var = jnp.mean(yc * yc, axis=-1, keepdims=True)
            g = g_ref[pl.ds(r0, ch), :].astype(F32)
            o_ref[pl.ds(r0, ch), :] = (yc * lax.rsqrt(var + EPS) * gnw_ref[...] * _silu(g)).astype(o_ref.dtype)
        return 0

    lax.fori_loop(0, n_ch // group, outputs, 0)


def _retention(proj_l, proj_c, cos_t, sin_t, dec, gnw):
    b, seq, _ = proj_l.shape
    ctx_len = proj_c.shape[1]
    assert ctx_len == RET_CHUNK and seq % (2 * RET_CHUNK) == 0
    n_ch = seq // RET_CHUNK
    kq, kv = RET_DK, RET_DV
    const = dict(pipeline_mode=pl.Buffered(1))
    return pl.pallas_call(
        _ret_kernel,
        out_shape=jax.ShapeDtypeStruct((b, seq, RET_V), BF16),
        grid=(b, RET_HEADS),
        in_specs=[pl.BlockSpec((None, seq, kq), lambda i, h: (i, 0, COL_Q // kq + h)),
                  pl.BlockSpec((None, seq, kq), lambda i, h: (i, 0, COL_K // kq + h)),
                  pl.BlockSpec((None, seq, kv), lambda i, h: (i, 0, COL_V // kv + h)),
                  pl.BlockSpec((None, seq, kv), lambda i, h: (i, 0, COL_G // kv + h)),
                  pl.BlockSpec((None, ctx_len, kq), lambda i, h: (i, 0, CTX_K // kq + h)),
                  pl.BlockSpec((None, ctx_len, kv), lambda i, h: (i, 0, CTX_V // kv + h)),
                  pl.BlockSpec((seq, kq), lambda i, h: (0, 0), **const),
                  pl.BlockSpec((seq, kq), lambda i, h: (0, 0), **const),
                  pl.BlockSpec((None, 2, 128), lambda i, h: (h, 0, 0)),
                  pl.BlockSpec((1, kv), lambda i, h: (0, h))],
        out_specs=pl.BlockSpec((None, seq, kv), lambda i, h: (i, 0, h)),
        scratch_shapes=[pltpu.VMEM((seq, kq), BF16), pltpu.VMEM((seq, 2 * kq), BF16),
                        pltpu.VMEM((seq, kq), BF16), pltpu.VMEM((seq, 2 * kq), BF16),
                        pltpu.VMEM((n_ch, RET_CHUNK + 2 * kq, kv), BF16),
                        pltpu.VMEM((kq, kv), F32), pltpu.VMEM((kq, kv), F32),
                        pltpu.VMEM((RET_CHUNK, RET_CHUNK), F32), pltpu.VMEM((4, RET_CHUNK, 128), F32)],
        compiler_params=pltpu.CompilerParams(dimension_semantics=("parallel", "arbitrary"), vmem_limit_bytes=VMEM_LIMIT),
        name="ret",
    )(proj_l, proj_l, proj_l, proj_l, proj_c, proj_c, cos_t, sin_t, dec, gnw)


def _ssd_kernel(xs_ref, b_ref, c_ref, z_ref, dt_ref, xsc_ref, bc_ref, dtc_ref,
                cwx_ref, cwb_ref, cwc_ref, cbx_ref, cbb_ref, cbc_ref, bias_ref, alog_ref, dsk_ref, nw_ref,
                shm_ref, tri_ref, emat_ref, hmask_ref, o_ref,
                pad_ref, xc_ref, xm_ref, bs_ref, cs_ref, dtv_ref, cum_ref, upd_ref, dec_ref, sall_ref,
                padc_ref, xcc_ref, bsc_ref, dtvc_ref, updc_ref, decc_ref, st_ref):
    seq = xs_ref.shape[0]
    ctx_len = xsc_ref.shape[0]
    q = SSD_CHUNK
    n_q = seq // q
    n_qc = ctx_len // q
    gw = SSD_GW
    ns = SSD_STATE
    ex = SSD_EXP
    tab = SSD_TAB
    halo = CONV_HALO

    lane_t = lax.broadcasted_iota(jnp.int32, (q, tab), 1)
    fwd_lane = (lane_t < gw) | ((lane_t >= ex) & (((lane_t - ex) & 7) < SSD_HPG))
    lane_c = lax.broadcasted_iota(jnp.int32, (q, 128), 1)
    piece = lane_c >> 3

    bias = bias_ref[...]
    neg_a = -jnp.exp(alog_ref[...]) * LOG2_E
    lane_h = lax.broadcasted_iota(jnp.int32, (q, 128), 1)

    def prep(pad, srcs, weights, biases, dts, nrows, dsts, dtv_dst, with_xm):
        width = pad.shape[1]
        pad[0:halo, :] = jnp.zeros((halo, width), BF16)
        pad[nrows + halo:nrows + 2 * halo, :] = jnp.zeros((halo, width), BF16)
        off = 0
        for s in srcs:
            w_ = s.shape[1]
            pad[halo:nrows + halo, off:off + w_] = s[...]
            off += w_
        cw = jnp.concatenate([w[...] for w in weights], axis=1)
        cb = jnp.concatenate([b_[...] for b_ in biases], axis=1)
        rows = CONV_ROWS
        half_w = SSD_CONV_W // 2
        strip_dst = []
        for dst in dsts:
            strip_dst += [(dst, o) for o in range(0, dst.shape[1], 128)]
        sub8 = lax.broadcasted_iota(jnp.int32, (8, 128), 0)

        def matmul_stage(i):
            r0 = pl.multiple_of(i * rows, rows)
            blk = pad[pl.ds(r0 + halo, rows), :]
            sh = jnp.dot(shm_ref[...], blk, preferred_element_type=F32)
            hi, lo = _split_bf16(_softplus(dts[pl.ds(r0, rows), :] + bias), 2)
            dtv_dst[pl.ds(r0, rows), :] = jnp.dot(jnp.concatenate([hi, lo], axis=1).astype(BF16), emat_ref[...],
                                                  preferred_element_type=F32)
            return r0, blk, sh

        def tap_stage(i, r0, blk, sh):
            prev = pad[pl.ds(r0, halo), :].astype(F32)[halo - 8:halo, :]
            nxt = pad[pl.ds(r0 + halo + rows, halo), :].astype(F32)[0:8, :]
            for s_, (dst, o) in enumerate(strip_dst):
                lo_, hi_ = s_ * 128, (s_ + 1) * 128
                acc = cb[:, lo_:hi_] + cw[half_w:half_w + 1, lo_:hi_] * blk[:, lo_:hi_].astype(F32)
                for n, off_ in enumerate(CONV_SHIFTS):
                    shf = sh[n * rows:(n + 1) * rows, lo_:hi_]
                    if off_ < 0:
                        edge = shf[0:8, :]
                        for e in range(-off_):
                            edge = jnp.where(sub8 == e, prev[8 + e + off_:9 + e + off_, lo_:hi_], edge)
                        shf = jnp.concatenate([edge, shf[8:, :]], axis=0)
                    else:
                        edge = shf[rows - 8:rows, :]
                        for e in range(off_):
                            edge = jnp.where(sub8 == 7 - e, nxt[off_ - 1 - e:off_ - e, lo_:hi_], edge)
                        shf = jnp.concatenate([shf[:rows - 8, :], edge], axis=0)
                    acc = acc + cw[off_ + half_w:off_ + half_w + 1, lo_:hi_] * shf
                y = _silu(acc)
                dst[pl.ds(r0, rows), o:o + 128] = y.astype(dst.dtype)
                if with_xm and dst is xc_ref:
                    for half in range(rows // q):
                        yh = y[half * q:(half + 1) * q, :]
                        for r in range(SSD_HPG):
                            row = pl.multiple_of(i * (rows // q) * SSD_HPG * q, q) + (half * SSD_HPG + r) * q
                            if r // 2 == s_:
                                keep = (lane_h < SSD_HEADDIM) if r % 2 == 0 else (lane_h >= SSD_HEADDIM)
                                xm_ref[pl.ds(row, q), o:o + 128] = jnp.where(keep, yh, 0.0).astype(BF16)
                            else:
                                xm_ref[pl.ds(row, q), o:o + 128] = jnp.zeros((q, 128), BF16)

        group = math.gcd(CONV_GROUP, nrows // rows)

        def conv(t, _):
            ids = [group * t + n for n in range(group)]
            staged = [matmul_stage(i) for i in ids]
            for i, st in zip(ids, staged):
                tap_stage(i, *st)
            return 0

        lax.fori_loop(0, nrows // rows // group, conv, 0)

    prep(pad_ref, (xs_ref, b_ref, c_ref), (cwx_ref, cwb_ref, cwc_ref), (cbx_ref, cbb_ref, cbc_ref),
         dt_ref, seq, (xc_ref, bs_ref, cs_ref), dtv_ref, True)
    prep(padc_ref, (xsc_ref, bc_ref), (cwx_ref, cwb_ref), (cbx_ref, cbb_ref),
         dtc_ref, ctx_len, (xcc_ref, bsc_ref), dtvc_ref, False)

    def tables(dtvs):
        a_s = [dtv * neg_a for dtv in dtvs]
        his, los = zip(*[_split_bf16(a, 2) for a in a_s])
        rhs = jnp.concatenate([jnp.concatenate(his, axis=1), jnp.concatenate(los, axis=1)], axis=0).astype(BF16)
        p_all = jnp.dot(tri_ref[...], rhs, preferred_element_type=F32)
        out = []
        for n, a in enumerate(a_s):
            p = p_all[:, n * tab:(n + 1) * tab]
            tot = p[q - 1:q, :]
            out.append((jnp.where(fwd_lane, p, tot - p + a), tot))
        return out

    def weighted_x(xcf, dtv, cum, tot):
        coef = dtv[:, 0:ex] * jnp.exp2(tot[:, 0:ex] - cum[:, 0:ex])
        return (jnp.concatenate([xcf, xcf], axis=1) * coef).astype(BF16)

    def contribution(bb, xw):
        return lax.dot_general(bb, xw, TN_DIMS, preferred_element_type=F32)

    def pass_a_pair(starts, load, cum_dst, upd_dst, dec_dst, chunk_ids):
        xcs, bbs, dtvs = zip(*[load(r0) for r0 in starts])
        tabs = tables(dtvs)
        xws = [weighted_x(xcf, dtv, cum, tot) for xcf, dtv, (cum, tot) in zip(xcs, dtvs, tabs)]
        for n, (r0, c) in enumerate(zip(starts, chunk_ids)):
            cum, tot = tabs[n]
            if cum_dst is not None:
                cum_dst[pl.ds(r0, q), :] = cum
            upd_dst[c] = contribution(bbs[n], xws[n])
            d0 = c * 8 if isinstance(c, int) else pl.multiple_of(c * 8, 8)
            dec_dst[pl.ds(d0, 8), :] = jnp.broadcast_to(jnp.exp2(tot[:, 0:ex]), (8, ex))

    assert n_qc == 2
    group_a = math.gcd(SSD_GROUP_A, n_q)
    group_c = math.gcd(SSD_GROUP_C, n_q)
    pass_a_pair([0, q], lambda r0: (xcc_ref[r0:r0 + q, :], bsc_ref[r0:r0 + q, :], dtvc_ref[r0:r0 + q, :]),
                None, updc_ref, decc_ref, [0, 1])

    def pass_a(grp_i, _):
        ids = [group_a * grp_i + n for n in range(group_a)]
        starts = [pl.multiple_of(c * q, q) for c in ids]
        pass_a_pair(starts, lambda r0: (xc_ref[pl.ds(r0, q), :], bs_ref[pl.ds(r0, q), :], dtv_ref[pl.ds(r0, q), :]),
                    cum_ref, upd_ref, dec_ref, ids)
        return 0

    lax.fori_loop(0, n_q // group_a, pass_a, 0)

    st_ref[...] = jnp.zeros((ns, ex), F32)
    for cc in range(n_qc):
        st_ref[:, 0:gw] = st_ref[:, 0:gw] * decc_ref[cc * 8:cc * 8 + 1, 0:gw] + updc_ref[cc, :, 0:gw]
    for cc in reversed(range(n_qc)):
        st_ref[:, gw:ex] = st_ref[:, gw:ex] * decc_ref[cc * 8:cc * 8 + 1, gw:ex] + updc_ref[cc, :, gw:ex]

    def pass_b(t, _):
        cf = t
        cb = n_q - 1 - t
        sall_ref[cf, :, 0:gw] = st_ref[:, 0:gw].astype(BF16)
        sall_ref[cb, :, gw:ex] = st_ref[:, gw:ex].astype(BF16)
        df = dec_ref[pl.ds(pl.multiple_of(cf * 8, 8), 8), :]
        db = dec_ref[pl.ds(pl.multiple_of(cb * 8, 8), 8), :]
        st_ref[:, 0:gw] = st_ref[:, 0:gw] * df[0:1, 0:gw] + upd_ref[cf, :, 0:gw]
        st_ref[:, gw:ex] = st_ref[:, gw:ex] * db[0:1, gw:ex] + upd_ref[cb, :, gw:ex]
        return 0

    lax.fori_loop(0, n_q, pass_b, 0)

    ii = lax.broadcasted_iota(jnp.int32, (q, q), 0)
    jj = lax.broadcasted_iota(jnp.int32, (q, q), 1)
    dsk = dsk_ref[...]
    nw = nw_ref[...]

    def exponents(r0):
        cum_c = cum_ref[pl.ds(r0, q), ex:tab]
        dt_c = dtv_ref[pl.ds(r0, q), ex:tab]
        ldt = jnp.where(dt_c > 0.0, jnp.log2(dt_c), NEG_BIG)
        c0, c1 = _split_bf16(cum_c, 2)
        l0, l1 = _split_bf16(ldt, 2)
        u = jnp.where(piece == 0, c0, jnp.where(piece == 1, c1, jnp.where(piece < 6, 1.0, 0.0)))
        v = jnp.where(piece < 2, 1.0, jnp.where(piece == 2, -c0, jnp.where(piece == 3, -c1, jnp.where(
            piece == 4, l0, jnp.where(piece == 5, l1, 0.0)))))
        vb = v.astype(BF16)
        vst = jnp.concatenate([vb * hmask_ref[m] for m in range(2 * SSD_HPG)], axis=0)
        return lax.dot_general(u.astype(BF16), vst, NT_DIMS, preferred_element_type=F32)

    def intra_weights(s, arg_all):
        lower = ii >= jj
        ws = []
        for r in range(SSD_HPG):
            arg = jnp.where(lower, arg_all[:, r * q:(r + 1) * q], arg_all[:, (SSD_HPG + r) * q:(SSD_HPG + r + 1) * q])
            ws.append((s * jnp.exp2(arg)).astype(BF16))
        return jnp.concatenate(ws, axis=1)

    def finish(r0, y, yi):
        ecum = jnp.exp2(cum_ref[pl.ds(r0, q), 0:ex])
        y = y + yi[:, 0:gw] * ecum[:, 0:gw] + yi[:, gw:ex] * ecum[:, gw:ex]
        y = (y + dsk * xc_ref[pl.ds(r0, q), :]) * _silu(z_ref[pl.ds(r0, q), :].astype(F32))
        y = y * lax.rsqrt(jnp.mean(y * y, axis=-1, keepdims=True) + EPS) * nw
        o_ref[pl.ds(r0, q), :] = y.astype(o_ref.dtype)

    def pass_c(grp_i, _):
        cs_ = [group_c * grp_i + n for n in range(group_c)]
        starts = [pl.multiple_of(c * q, q) for c in cs_]
        args = [exponents(r0) for r0 in starts]
        cbs = [cs_ref[pl.ds(r0, q), :] for r0 in starts]
        scores = [lax.dot_general(cb, bs_ref[pl.ds(r0, q), :], NT_DIMS, preferred_element_type=F32)
                  for cb, r0 in zip(cbs, starts)]
        yis = [jnp.dot(cb, sall_ref[c], preferred_element_type=F32) for cb, c in zip(cbs, cs_)]
        wcat = [intra_weights(s, a) for s, a in zip(scores, args)]
        ys = [jnp.dot(w, xm_ref[pl.ds(pl.multiple_of(c * SSD_HPG * q, q), SSD_HPG * q), :], preferred_element_type=F32)
              for w, c in zip(wcat, cs_)]
        for r0, y, yi in zip(starts, ys, yis):
            finish(r0, y, yi)
        return 0

    lax.fori_loop(0, n_q // group_c, pass_c, 0)


def _ssd_constants():
    rows, q, tab, ex = CONV_ROWS, SSD_CHUNK, SSD_TAB, SSD_EXP
    t = np.arange(rows)
    shm = np.concatenate([(t[None, :] == t[:, None] + o) for o in CONV_SHIFTS], axis=0)
    tq = np.arange(q)
    tri = np.tile(tq[None, :] <= tq[:, None], (1, 2))
    lane = np.arange(tab)
    want = np.where(lane < ex, 4 * (lane // SSD_GW) + (lane % SSD_GW) // SSD_HEADDIM,
                    np.where(lane < ex + 64, (lane - ex) % 8, -1000))
    src = np.arange(2 * DT_LANES) % DT_LANES
    emat = np.stack([(src[:, None] - 8 * g) == want[None, :] for g in range(SSD_GROUPS)], axis=0)
    as_bf16 = lambda a: jnp.asarray(a.astype(np.float32), dtype=BF16)
    hmask = np.broadcast_to((np.arange(128) % 8)[None, None, :] == np.arange(2 * SSD_HPG)[:, None, None],
                            (2 * SSD_HPG, q, 128))
    return as_bf16(shm), as_bf16(tri), as_bf16(emat), as_bf16(hmask)


def _ssd(proj_l, proj_c, dt_l, dt_c, conv_w, conv_b, bias, alog, dsk, nw):
    b, seq, _ = proj_l.shape
    ctx_len = proj_c.shape[1]
    assert seq % CONV_ROWS == 0 and ctx_len % CONV_ROWS == 0 and CONV_ROWS % SSD_CHUNK == 0
    gw, ns, q, ex, tab = SSD_GW, SSD_STATE, SSD_CHUNK, SSD_EXP, SSD_TAB
    n_q, n_qc = seq // q, ctx_len // q
    cx, cb_, cc_ = 0, SSD_INNER // ns, (SSD_INNER + SSD_BC) // ns
    shm, tri, emat, hmask = _ssd_constants()
    return pl.pallas_call(
        _ssd_kernel,
        out_shape=jax.ShapeDtypeStruct((b, seq, SSD_INNER), BF16),
        grid=(b, SSD_GROUPS),
        in_specs=[pl.BlockSpec((None, seq, gw), lambda i, g: (i, 0, COL_XS // gw + g)),
                  pl.BlockSpec((None, seq, ns), lambda i, g: (i, 0, COL_B // ns + g)),
                  pl.BlockSpec((None, seq, ns), lambda i, g: (i, 0, COL_C // ns + g)),
                  pl.BlockSpec((None, seq, gw), lambda i, g: (i, 0, COL_Z // gw + g)),
                  pl.BlockSpec((None, seq, DT_LANES), lambda i, g: (i, 0, 0)),
                  pl.BlockSpec((None, ctx_len, gw), lambda i, g: (i, 0, CTX_XS // gw + g)),
                  pl.BlockSpec((None, ctx_len, ns), lambda i, g: (i, 0, CTX_B // ns + g)),
                  pl.BlockSpec((None, ctx_len, DT_LANES), lambda i, g: (i, 0, 0)),
                  pl.BlockSpec((SSD_CONV_W, gw), lambda i, g: (0, cx + g)),
                  pl.BlockSpec((SSD_CONV_W, ns), lambda i, g: (0, cb_ + g)),
                  pl.BlockSpec((SSD_CONV_W, ns), lambda i, g: (0, cc_ + g)),
                  pl.BlockSpec((1, gw), lambda i, g: (0, cx + g)),
                  pl.BlockSpec((1, ns), lambda i, g: (0, cb_ + g)),
                  pl.BlockSpec((1, ns), lambda i, g: (0, cc_ + g)),
                  pl.BlockSpec((1, DT_LANES), lambda i, g: (0, 0)),
                  pl.BlockSpec((None, 1, tab), lambda i, g: (g, 0, 0)),
                  pl.BlockSpec((1, gw), lambda i, g: (0, g)),
                  pl.BlockSpec((1, gw), lambda i, g: (0, g)),
                  pl.BlockSpec(shm.shape, lambda i, g: (0, 0)),
                  pl.BlockSpec(tri.shape, lambda i, g: (0, 0)),
                  pl.BlockSpec((None,) + emat.shape[1:], lambda i, g: (g, 0, 0)),
                  pl.BlockSpec(hmask.shape, lambda i, g: (0, 0, 0))],
        out_specs=pl.BlockSpec((None, seq, gw), lambda i, g: (i, 0, g)),
        scratch_shapes=[pltpu.VMEM((seq + 2 * CONV_HALO, gw + 2 * ns), BF16),
                        pltpu.VMEM((seq, gw), F32),
                        pltpu.VMEM((n_q * SSD_HPG * q, gw), BF16),
                        pltpu.VMEM((seq, ns), BF16), pltpu.VMEM((seq, ns), BF16),
                        pltpu.VMEM((seq, tab), F32), pltpu.VMEM((seq, tab), F32),
                        pltpu.VMEM((n_q, ns, ex), F32), pltpu.VMEM((n_q * 8, ex), F32),
                        pltpu.VMEM((n_q, ns, ex), BF16),
                        pltpu.VMEM((ctx_len + 2 * CONV_HALO, gw + ns), BF16),
                        pltpu.VMEM((ctx_len, gw), F32), pltpu.VMEM((ctx_len, ns), BF16),
                        pltpu.VMEM((ctx_len, tab), F32),
                        pltpu.VMEM((n_qc, ns, ex), F32), pltpu.VMEM((n_qc * 8, ex), F32),
                        pltpu.VMEM((ns, ex), F32)],
        compiler_params=pltpu.CompilerParams(dimension_semantics=("parallel", "arbitrary"), vmem_limit_bytes=VMEM_LIMIT),
        name="ssd",
    )(proj_l, proj_l, proj_l, proj_l, dt_l, proj_c, proj_c, dt_c,
      conv_w, conv_w, conv_w, conv_b, conv_b, conv_b, bias, alog, dsk, nw, shm, tri, emat, hmask)


def _merge_kernel(yr_ref, ys_ref, gt_ref, x_ref, gate_ref, wr_ref, ws_ref, wo_ref, nw_ref, o_ref):
    o_r = jnp.dot(yr_ref[...], wr_ref[...], preferred_element_type=F32)
    o_s = jnp.dot(ys_ref[...], ws_ref[...], preferred_element_type=F32)
    gt = gt_ref[...].astype(F32)
    m = jax.nn.sigmoid(gt[:, :D_MODEL]) * o_r + jax.nn.sigmoid(gt[:, D_MODEL:]) * o_s
    out = jnp.dot(m.astype(BF16), wo_ref[...], preferred_element_type=F32)
    nrm = out * lax.rsqrt(jnp.mean(out * out, axis=-1, keepdims=True) + EPS) * nw_ref[...]
    o_ref[...] = x_ref[...] + gate_ref[...] * nrm


def _merge(yr, ys, proj_l, x2, gate, w_ret_o, w_ssd_o, w_out, nw, seq_len):
    m = x2.shape[0]
    tm = min(512, seq_len)
    per = seq_len // tm
    const = dict(pipeline_mode=pl.Buffered(1))
    return pl.pallas_call(
        _merge_kernel,
        out_shape=jax.ShapeDtypeStruct((m, D_MODEL), F32),
        grid=(m // tm,),
        in_specs=[pl.BlockSpec((tm, RET_V), lambda i: (i, 0)),
                  pl.BlockSpec((tm, SSD_INNER), lambda i: (i, 0)),
                  pl.BlockSpec((tm, 2 * D_MODEL), lambda i: (i, COL_GATES // (2 * D_MODEL))),
                  pl.BlockSpec((tm, D_MODEL), lambda i: (i, 0)),
                  pl.BlockSpec((None, 1, D_MODEL), lambda i: (i // per, 0, 0)),
                  pl.BlockSpec((RET_V, D_MODEL), lambda i: (0, 0), **const),
                  pl.BlockSpec((SSD_INNER, D_MODEL), lambda i: (0, 0), **const),
                  pl.BlockSpec((D_MODEL, D_MODEL), lambda i: (0, 0), **const),
                  pl.BlockSpec((1, D_MODEL), lambda i: (0, 0))],
        out_specs=pl.BlockSpec((tm, D_MODEL), lambda i: (i, 0)),
        compiler_params=pltpu.CompilerParams(dimension_semantics=("parallel",), vmem_limit_bytes=VMEM_LIMIT),
        name="merge",
    )(yr, ys, proj_l, x2, gate, w_ret_o, w_ssd_o, w_out, nw)


def _rope_tables(n_tokens):
    pos = jnp.arange(n_tokens)
    row = (pos // GRID_W).astype(F32)
    col = (pos % GRID_W).astype(F32)
    inv_freq = ROPE_THETA ** (-jnp.arange(ROPE_FREQS, dtype=F32) / ROPE_FREQS)
    ar = row[:, None] * inv_freq
    ac = col[:, None] * inv_freq
    cos_t = jnp.concatenate([jnp.cos(ar), jnp.cos(ar), jnp.cos(ac), jnp.cos(ac)], axis=1)
    sin_t = jnp.concatenate([-jnp.sin(ar), jnp.sin(ar), -jnp.sin(ac), jnp.sin(ac)], axis=1)
    return cos_t, sin_t


def _group_major(t):
    lead = t.shape[:-2]
    t = t.reshape(lead + (2, SSD_GROUPS, SSD_HPG))
    t = jnp.moveaxis(t, -3, -2)
    return t.reshape(lead + (SSD_GROUPS, 2 * SSD_HPG))


def _table_lanes(t):
    gm = _group_major(t)
    expanded = jnp.repeat(gm, SSD_HEADDIM, axis=1)
    compact = jnp.pad(jnp.tile(gm, (1, 8)), ((0, 0), (0, 64)))
    return jnp.concatenate([expanded, compact], axis=1)[:, None, :]


def kernel(x, c, ctx, c_ctx, w_mod, b_mod, norm_pre_w, norm_post_w, w_in, ret_decay, ret_gn_w, ssd_conv_w, ssd_conv_b,
           ssd_dt_bias, ssd_a_log, ssd_D, ssd_norm_w, w_ret_o, w_ssd_o, w_out):
    assert w_in.shape[0] == 1, "single-layer problem: the context stream is never updated"
    b, seq, d = x.shape
    ctx_len = ctx.shape[1]

    offs = [0]
    for s in IN_SIZES:
        offs.append(offs[-1] + s)
    dt_lo, dt_hi = offs[6], offs[7]
    w_main = jnp.concatenate([w_in[0][:, :dt_lo], w_in[0][:, dt_hi:]], axis=1).astype(BF16)
    wdt = _group_major(w_in[0][:, dt_lo:dt_hi].reshape(d, 2, SSD_HEADS)).reshape(d, 2 * SSD_HEADS)
    wdt = jnp.pad(wdt, ((0, 0), (0, DT_LANES - 2 * SSD_HEADS))).astype(BF16)
    bias = jnp.pad(_group_major(ssd_dt_bias[0]).reshape(1, 2 * SSD_HEADS), ((0, 0), (0, DT_LANES - 2 * SSD_HEADS)))
    alog = _table_lanes(ssd_a_log[0])
    dsk = jnp.repeat(ssd_D[0], SSD_HEADDIM)[None, :]
    dec = jnp.broadcast_to(ret_decay[0].T[:, :, None], (RET_HEADS, 2, 128))

    n_rows = -(-(b + 1) // 8) * 8
    c_all = jnp.concatenate([c, c_ctx[None, :], jnp.zeros((n_rows - b - 1, d), F32)], axis=0)
    mod = _modulation(c_all, w_mod[0], b_mod[0][None, :])
    shift, scale, gate = mod[:, :d], mod[:, d:2 * d], mod[:, 2 * d:]
    nw_pre = norm_pre_w[0][None, :]

    x2 = x.reshape(b * seq, d)
    proj_l, dt_l = _inproj(x2, shift[:b, None, :], scale[:b, None, :], nw_pre, w_main, wdt, seq, N_MAIN, "inproj_latent")
    proj_c, dt_c = _inproj(ctx.reshape(b * ctx_len, d), shift[b:b + 1, None, :], scale[b:b + 1, None, :], nw_pre,
                           w_main, wdt, ctx_len, N_CTX, "inproj_ctx")
    proj_l = proj_l.reshape(b, seq, N_MAIN)
    proj_c = proj_c.reshape(b, ctx_len, N_CTX)
    dt_l = dt_l.reshape(b, seq, DT_LANES)
    dt_c = dt_c.reshape(b, ctx_len, DT_LANES)

    cos_t, sin_t = _rope_tables(seq)
    yr = _retention(proj_l, proj_c, cos_t, sin_t, dec, ret_gn_w[0][None, :])
    ys = _ssd(proj_l, proj_c, dt_l, dt_c, ssd_conv_w[0], ssd_conv_b[0][None, :], bias, alog, dsk, ssd_norm_w[0][None, :])

    out = _merge(yr.reshape(b * seq, RET_V), ys.reshape(b * seq, SSD_INNER), proj_l.reshape(b * seq, N_MAIN), x2,
                 gate[:b, None, :], w_ret_o[0].astype(BF16), w_ssd_o[0].astype(BF16), w_out[0].astype(BF16),
                 norm_post_w[0][None, :], seq)
    return out.reshape(b, seq, d)
```
